```python
import math
import jax, jax.numpy as jnp
from jax import lax
import numpy as np

D_MODEL = 1024
BATCH = 8
SEQ = 4096
DEPTH = 1

EPS = 1e-6
MEM_LEN = 256
RET_HEADS = 4
RET_DK = 128
RET_DV = 128
RET_QK_WIDTH = RET_HEADS * RET_DK
RET_V_WIDTH = RET_HEADS * RET_DV
RET_CHUNK = 128
ROPE_BASE = 10000.0
S5_GROUPS = 16
S5_GROUP_DIM = 16
S5_STATE = 64
S5_WIDTH = S5_GROUPS * S5_GROUP_DIM
S5_DT_MIN = 0.001
S5_DT_MAX = 0.1
MEM_HEADS = 4
MEM_HEAD_DIM = 64
MEM_WIDTH = MEM_HEADS * MEM_HEAD_DIM
N_BRANCHES = 3
IN_SPLITS = (RET_QK_WIDTH, RET_QK_WIDTH, RET_V_WIDTH, RET_V_WIDTH, S5_WIDTH, MEM_WIDTH, N_BRANCHES * D_MODEL)
IN_COLS = sum(IN_SPLITS)
PEER_HEADS = 8
PEER_NKEYS = 128
PEER_EXPERTS = PEER_NKEYS * PEER_NKEYS
PEER_QDIM = 256
PEER_HALF = PEER_QDIM // 2
PEER_TOPK = 16
PEER_BLOCK = 128

kernel_name = "hybrid_retention_s5_memattn_peer_block"


def _rms_norm(x, gain):
    xf = x.astype(jnp.float32)
    y = xf * lax.rsqrt(jnp.mean(xf * xf, axis=-1, keepdims=True) + EPS)
    return (y * gain.astype(jnp.float32)).astype(x.dtype)


def _rope(x, positions):
    half = x.shape[-1] // 2
    inv_freq = ROPE_BASE ** (-jnp.arange(half, dtype=jnp.float32) / half)
    ang = positions.astype(jnp.float32)[..., None] * inv_freq
    cos = jnp.cos(ang)[:, :, None, :]
    sin = jnp.sin(ang)[:, :, None, :]
    xf = x.astype(jnp.float32)
    x1, x2 = xf[..., :half], xf[..., half:]
    return jnp.concatenate([x1 * cos - x2 * sin, x1 * sin + x2 * cos], axis=-1)


def _retention(q, k, v):
    bsz, seq, nh, dk = q.shape
    dv = v.shape[-1]
    c = RET_CHUNK
    nc = seq // c
    log_gamma = jnp.log1p(-jnp.exp2(-5.0 - jnp.arange(nh, dtype=jnp.float32)))
    pos = jnp.arange(c, dtype=jnp.float32)
    rel = pos[:, None] - pos[None, :]
    decay_intra = jnp.exp(jnp.where((rel >= 0)[None], rel[None] * log_gamma[:, None, None], -jnp.inf))
    q = q.reshape(bsz, nc, c, nh, dk)
    k = k.reshape(bsz, nc, c, nh, dk)
    v = v.reshape(bsz, nc, c, nh, dv)
    scores = jnp.einsum("bnihd,bnjhd->bnhij", q, k) * decay_intra
    inner = jnp.einsum("bnhij,bnjhe->bnihe", scores, v)
    k_dec = k * jnp.exp((c - 1.0 - pos)[:, None] * log_gamma[None, :])[:, :, None]
    kv = jnp.einsum("bnjhd,bnjhe->nbhde", k_dec, v)
    chunk_decay = jnp.exp(c * log_gamma)[None, :, None, None]

    def step(state, kv_n):
        return state * chunk_decay + kv_n, state

    _, prev = lax.scan(step, jnp.zeros((bsz, nh, dk, dv), jnp.float32), kv)
    q_dec = q * jnp.exp((pos + 1.0)[:, None] * log_gamma[None, :])[:, :, None]
    cross = jnp.einsum("bnihd,nbhde->bnihe", q_dec, prev)
    return (inner + cross).reshape(bsz, seq, nh, dv)


def _complex_linear_op(e1, e2):
    ar1, ai1, br1, bi1 = e1
    ar2, ai2, br2, bi2 = e2
    return (ar2 * ar1 - ai2 * ai1,
            ar2 * ai1 + ai2 * ar1,
            ar2 * br1 - ai2 * bi1 + br2,
            ar2 * bi1 + ai2 * br1 + bi2)


def _s5(u, a_re, a_im, log_dt, b_re, b_im, c_re, c_im, d, w_glu):
    bsz, seq, _ = u.shape
    f32 = jnp.float32
    uf = u.astype(f32).reshape(bsz, seq, S5_GROUPS, S5_GROUP_DIM)
    a_re = a_re.astype(f32)
    a_im = a_im.astype(f32)
    dt = jnp.exp(log_dt.astype(f32))[:, None]
    mag = jnp.exp(a_re * dt)
    lam_re = mag * jnp.cos(a_im * dt)
    lam_im = mag * jnp.sin(a_im * dt)
    den = a_re * a_re + a_im * a_im
    nr = lam_re - 1.0
    ni = lam_im
    cr = ((nr * a_re + ni * a_im) / den)[..., None]
    ci = ((ni * a_re - nr * a_im) / den)[..., None]
    b_re = b_re.astype(f32)
    b_im = b_im.astype(f32)
    bb_re = cr * b_re - ci * b_im
    bb_im = cr * b_im + ci * b_re
    bu_re = jnp.einsum("bsgh,gph->bsgp", uf, bb_re)
    bu_im = jnp.einsum("bsgh,gph->bsgp", uf, bb_im)
    la_re = jnp.broadcast_to(lam_re, bu_re.shape)
    la_im = jnp.broadcast_to(lam_im, bu_im.shape)
    _, _, x_re, x_im = lax.associative_scan(_complex_linear_op, (la_re, la_im, bu_re, bu_im), axis=1)
    y = (jnp.einsum("bsgp,ghp->bsgh", x_re, c_re.astype(f32))
         - jnp.einsum("bsgp,ghp->bsgh", x_im, c_im.astype(f32)))
    y = y.reshape(bsz, seq, S5_WIDTH) + d.astype(f32) * u.astype(f32)
    y = jax.nn.gelu(y)
    lin, gate = jnp.split(y @ w_glu.astype(f32), 2, axis=-1)
    return lin * jax.nn.sigmoid(gate)


def _memory_attention(q, k, v):
    bsz, seq = q.shape[0], q.shape[1]
    scores = jnp.einsum("bshd,bmhd->bhsm", q.astype(jnp.float32), k.astype(jnp.float32)) * (MEM_HEAD_DIM ** -0.5)
    p = jax.nn.softmax(scores, axis=-1)
    out = jnp.einsum("bhsm,bmhd->bshd", p, v.astype(jnp.float32))
    return out.reshape(bsz, seq, MEM_WIDTH)


def _peer(xn, w_q, keys, u_exp, v_exp):
    bsz, seq, dm = xn.shape
    t = bsz * seq
    xt = xn.reshape(t, dm)
    q = (xt @ w_q).reshape(t, PEER_HEADS, 2, PEER_HALF)
    s = jnp.einsum("thcd,hcnd->thcn", q, keys).astype(jnp.float32)
    s_top, i_top = lax.top_k(s, PEER_TOPK)
    cand_s = (s_top[:, :, 0, :, None] + s_top[:, :, 1, None, :]).reshape(t, PEER_HEADS, PEER_TOPK * PEER_TOPK)
    cand_i = (i_top[:, :, 0, :, None] * PEER_NKEYS + i_top[:, :, 1, None, :]).reshape(t, PEER_HEADS, PEER_TOPK * PEER_TOPK)
    s_fin, sel = lax.top_k(cand_s, PEER_TOPK)
    idx = jnp.take_along_axis(cand_i, sel, axis=-1)
    w = jax.nn.softmax(s_fin, axis=-1)
    nb = t // PEER_BLOCK

    def expert_block(args):
        xb, ib, wb = args
        ug = jnp.take(u_exp, ib, axis=0)
        vg = jnp.take(v_exp, ib, axis=0)
        hid = jnp.einsum("td,thkd->thk", xb, ug).astype(jnp.float32)
        act = jax.nn.gelu(hid) * wb
        return jnp.einsum("thk,thkd->td", act, vg.astype(jnp.float32))

    out = lax.map(expert_block, (xt.reshape(nb, PEER_BLOCK, dm),
                                 idx.reshape(nb, PEER_BLOCK, PEER_HEADS, PEER_TOPK),
                                 w.reshape(nb, PEER_BLOCK, PEER_HEADS, PEER_TOPK)))
    return out.reshape(bsz, seq, dm)


def setup_inputs(seed: int = 0) -> dict:
    key = jax.random.key(seed)
    ks = jax.random.split(key, 32)
    f32 = jnp.float32

    def nrm(k, shape, scale):
        return jax.random.normal(k, shape, f32) * scale

    def gain(k, shape):
        return 1.0 + 0.01 * jax.random.normal(k, shape, f32)

    x = nrm(ks[0], (BATCH, SEQ, D_MODEL), 1.0)
    mem = nrm(ks[1], (BATCH, MEM_LEN, D_MODEL), 1.0)
    positions = jnp.arange(SEQ, dtype=jnp.int32)[None, :] + jax.random.randint(ks[2], (BATCH, 1), 0, SEQ, dtype=jnp.int32)
    norm_mix = gain(ks[3], (DEPTH, D_MODEL))
    w_in = nrm(ks[4], (DEPTH, D_MODEL, IN_COLS), D_MODEL ** -0.5)
    norm_mem = gain(ks[5], (DEPTH, D_MODEL))
    w_mem_kv = nrm(ks[6], (DEPTH, D_MODEL, 2 * MEM_WIDTH), D_MODEL ** -0.5)
    s5_a_re = -0.5 + 0.01 * jax.random.normal(ks[7], (DEPTH, S5_GROUPS, S5_STATE), f32)
    s5_a_im = math.pi * jnp.arange(S5_STATE, dtype=f32) + 0.01 * jax.random.normal(ks[8], (DEPTH, S5_GROUPS, S5_STATE), f32)
    s5_log_dt = jax.random.uniform(ks[9], (DEPTH, S5_GROUPS), f32, minval=math.log(S5_DT_MIN), maxval=math.log(S5_DT_MAX))
    s5_b_re = nrm(ks[10], (DEPTH, S5_GROUPS, S5_STATE, S5_GROUP_DIM), (2 * S5_GROUP_DIM) ** -0.5)
    s5_b_im = nrm(ks[11], (DEPTH, S5_GROUPS, S5_STATE, S5_GROUP_DIM), (2 * S5_GROUP_DIM) ** -0.5)
    s5_c_re = nrm(ks[12], (DEPTH, S5_GROUPS, S5_GROUP_DIM, S5_STATE), S5_STATE ** -0.5)
    s5_c_im = nrm(ks[13], (DEPTH, S5_GROUPS, S5_GROUP_DIM, S5_STATE), S5_STATE ** -0.5)
    s5_d = nrm(ks[14], (DEPTH, S5_WIDTH), 1.0)
    s5_w_glu = nrm(ks[15], (DEPTH, S5_WIDTH, 2 * S5_WIDTH), S5_WIDTH ** -0.5)
    w_branch_ret = nrm(ks[16], (DEPTH, RET_V_WIDTH, D_MODEL), RET_V_WIDTH ** -0.5)
    w_branch_s5 = nrm(ks[17], (DEPTH, S5_WIDTH, D_MODEL), S5_WIDTH ** -0.5)
    w_branch_mem = nrm(ks[18], (DEPTH, MEM_WIDTH, D_MODEL), MEM_WIDTH ** -0.5)
    w_out = nrm(ks[19], (DEPTH, D_MODEL, D_MODEL), D_MODEL ** -0.5)
    norm_ffn = gain(ks[20], (DEPTH, D_MODEL))
    peer_w_q = nrm(ks[21], (DEPTH, D_MODEL, PEER_HEADS * PEER_QDIM), D_MODEL ** -0.5)
    peer_keys = nrm(ks[22], (DEPTH, PEER_HEADS, 2, PEER_NKEYS, PEER_HALF), PEER_HALF ** -0.5)
    peer_u = nrm(ks[23], (DEPTH, PEER_EXPERTS, D_MODEL), D_MODEL ** -0.5)
    peer_v = nrm(ks[24], (DEPTH, PEER_EXPERTS, D_MODEL), PEER_HEADS ** -0.5)
    norm_final = gain(ks[25], (D_MODEL,))
    return {"x": x, "mem": mem, "positions": positions, "norm_mix": norm_mix, "w_in": w_in,
            "norm_mem": norm_mem, "w_mem_kv": w_mem_kv, "s5_a_re": s5_a_re, "s5_a_im": s5_a_im,
            "s5_log_dt": s5_log_dt, "s5_b_re": s5_b_re, "s5_b_im": s5_b_im, "s5_c_re": s5_c_re,
            "s5_c_im": s5_c_im, "s5_d": s5_d, "s5_w_glu": s5_w_glu, "w_branch_ret": w_branch_ret,
            "w_branch_s5": w_branch_s5, "w_branch_mem": w_branch_mem, "w_out": w_out,
            "norm_ffn": norm_ffn, "peer_w_q": peer_w_q, "peer_keys": peer_keys, "peer_u": peer_u,
            "peer_v": peer_v, "norm_final": norm_final}


def reference(x, mem, positions, norm_mix, w_in, norm_mem, w_mem_kv, s5_a_re, s5_a_im,
              s5_log_dt, s5_b_re, s5_b_im, s5_c_re, s5_c_im, s5_d, s5_w_glu, w_branch_ret,
              w_branch_s5, w_branch_mem, w_out, norm_ffn, peer_w_q, peer_keys, peer_u,
              peer_v, norm_final):
    bsz, seq, _ = x.shape
    offsets = [int(o) for o in np.cumsum(IN_SPLITS)[:-1]]
    h = x
    for l in range(DEPTH):
        n = _rms_norm(h, norm_mix[l])
        z = n @ w_in[l]
        q_r, k_r, v_r, g_r, u_s5, q_m, gates = jnp.split(z, offsets, axis=-1)
        q_r = _rope(q_r.reshape(bsz, seq, RET_HEADS, RET_DK), positions)
        k_r = _rope(k_r.reshape(bsz, seq, RET_HEADS, RET_DK), positions) * (RET_DK ** -0.5)
        v_r = v_r.reshape(bsz, seq, RET_HEADS, RET_DV).astype(jnp.float32)
        ret = _retention(q_r, k_r, v_r)
        ret = ret * lax.rsqrt(jnp.mean(ret * ret, axis=-1, keepdims=True) + EPS)
        ret = ret.reshape(bsz, seq, RET_V_WIDTH) * jax.nn.silu(g_r.astype(jnp.float32))
        ssm = _s5(u_s5, s5_a_re[l], s5_a_im[l], s5_log_dt[l], s5_b_re[l], s5_b_im[l],
                  s5_c_re[l], s5_c_im[l], s5_d[l], s5_w_glu[l])
        mem_n = _rms_norm(mem, norm_mem[l])
        k_m, v_m = jnp.split(mem_n @ w_mem_kv[l], 2, axis=-1)
        mattn = _memory_attention(q_m.reshape(bsz, seq, MEM_HEADS, MEM_HEAD_DIM),
                                  k_m.reshape(bsz, MEM_LEN, MEM_HEADS, MEM_HEAD_DIM),
                                  v_m.reshape(bsz, MEM_LEN, MEM_HEADS, MEM_HEAD_DIM))
        gate_r, gate_s, gate_m = jnp.split(jax.nn.sigmoid(gates.astype(jnp.float32)), N_BRANCHES, axis=-1)
        merged = (gate_r * (ret @ w_branch_ret[l])
                  + gate_s * (ssm @ w_branch_s5[l])
                  + gate_m * (mattn @ w_branch_mem[l]))
        h = h + (merged @ w_out[l]).astype(h.dtype)
        h = h + _peer(_rms_norm(h, norm_ffn[l]), peer_w_q[l], peer_keys[l], peer_u[l], peer_v[l]).astype(h.dtype)
    return _rms_norm(h, norm_final)
```

```python
import functools
import math

import jax
import jax.numpy as jnp
from jax import lax
from jax.experimental import pallas as pl
from jax.experimental.pallas import tpu as pltpu

F32 = jnp.float32
BF16 = jnp.bfloat16

EPS = 1e-6
RET_HEADS = 4
RET_DK = 128
RET_DV = 128
RET_QK_WIDTH = RET_HEADS * RET_DK
RET_V_WIDTH = RET_HEADS * RET_DV
ROPE_BASE = 10000.0
S5_GROUPS = 16
S5_GROUP_DIM = 16
S5_STATE = 64
S5_WIDTH = S5_GROUPS * S5_GROUP_DIM
S5_NSTATE = S5_GROUPS * S5_STATE
MEM_HEADS = 4
MEM_HEAD_DIM = 64
MEM_WIDTH = MEM_HEADS * MEM_HEAD_DIM
PEER_HEADS = 8
PEER_NKEYS = 128
PEER_HALF = 128
PEER_TOPK = 16

LANES = 128
SUBLANES = 8
VMEM_LIMIT = 48 * 1024 * 1024
PEER_VMEM_LIMIT = 56 * 1024 * 1024

COL_Q, COL_K, COL_V, COL_G = 0, 512, 1024, 1536
COL_U, COL_QM, COL_GATES = 2048, 2304, 2560
IN_COLS = 5632


def _cparams(n_axes):
    return pltpu.CompilerParams(dimension_semantics=("arbitrary",) * n_axes, vmem_limit_bytes=VMEM_LIMIT)


def _rms(x, gain):
    return x * lax.rsqrt(jnp.mean(x * x, axis=-1, keepdims=True) + EPS) * gain


def _gelu(x):
    c = math.sqrt(2.0 / math.pi)
    return 0.5 * x * (1.0 + jnp.tanh(c * (x + 0.044715 * (x * x * x))))


def _sigmoid(x):
    return 1.0 / (1.0 + jnp.exp(-x))


def _s5_disc_kernel(are_ref, aim_ref, ldt_ref, bre_ref, bim_ref, lre_ref, lim_ref, bbre_ref, bbim_ref):
    a_re = are_ref[...]
    a_im = aim_ref[...]
    dt = jnp.exp(ldt_ref[...])
    mag = jnp.exp(a_re * dt)
    lam_re = mag * jnp.cos(a_im * dt)
    lam_im = mag * jnp.sin(a_im * dt)
    den = a_re * a_re + a_im * a_im
    nr = lam_re - 1.0
    ni = lam_im
    cr = (nr * a_re + ni * a_im) / den
    ci = (ni * a_re - nr * a_im) / den
    b_re = bre_ref[...]
    b_im = bim_ref[...]
    lre_ref[...] = lam_re
    lim_ref[...] = lam_im
    bbre_ref[...] = cr * b_re - ci * b_im
    bbim_ref[...] = cr * b_im + ci * b_re


def _s5_discretize(a_re, a_im, log_dt, b_re, b_im):
    g, p = a_re.shape
    h = b_re.shape[-1]
    rep = lambda a: jnp.repeat(a, h, axis=0)
    b_re_t = jnp.transpose(b_re, (0, 2, 1)).reshape(g * h, p)
    b_im_t = jnp.transpose(b_im, (0, 2, 1)).reshape(g * h, p)
    shp = jax.ShapeDtypeStruct((g * h, p), F32)
    lam_re, lam_im, bb_re, bb_im = pl.pallas_call(
        _s5_disc_kernel, out_shape=(shp, shp, shp, shp),
    )(rep(a_re), rep(a_im), rep(log_dt[:, None]), b_re_t, b_im_t)
    return lam_re[::h], lam_im[::h], bb_re.reshape(g, h, p), bb_im.reshape(g, h, p)


def _inproj_kernel(x_ref, g_ref, w_ref, z_ref, xn_ref):
    @pl.when(pl.program_id(1) == 0)
    def _():
        xn_ref[...] = _rms(x_ref[...], g_ref[...]).astype(BF16)

    z_ref[...] = jnp.dot(xn_ref[...], w_ref[...], preferred_element_type=F32)


def _inproj(x2, gain, w_bf):
    t, d = x2.shape
    tm, tn = 1024, 512
    return pl.pallas_call(
        _inproj_kernel,
        grid=(t // tm, IN_COLS // tn),
        in_specs=[pl.BlockSpec((tm, d), lambda i, j: (i, 0)),
                  pl.BlockSpec((1, d), lambda i, j: (0, 0)),
                  pl.BlockSpec((d, tn), lambda i, j: (0, j))],
        out_specs=pl.BlockSpec((tm, tn), lambda i, j: (i, j)),
        out_shape=jax.ShapeDtypeStruct((t, IN_COLS), F32),
        scratch_shapes=[pltpu.VMEM((tm, d), BF16)],
        compiler_params=_cparams(2),
    )(x2, gain, w_bf)


def _ret_log_gamma(h):
    return math.log1p(-(2.0 ** (-5.0 - h)))


def _retention_kernel(q_ref, k_ref, v_ref, g_ref, pos_ref, invf_ref, o_ref,
                      state_ref, dmat_ref, qdec_ref, kdec_ref, *, lc):
    b = pl.program_id(0)
    c = pl.program_id(1)

    @pl.when((b == 0) & (c == 0))
    def _():
        row = lax.broadcasted_iota(jnp.int32, (lc, lc), 0)
        col = lax.broadcasted_iota(jnp.int32, (lc, lc), 1)
        rel = (row - col).astype(F32)
        pos = lax.broadcasted_iota(jnp.int32, (lc, RET_DK), 0).astype(F32)
        for h in range(RET_HEADS):
            lg = _ret_log_gamma(h)
            dmat_ref[h] = jnp.where(rel >= 0.0, jnp.exp(rel * lg), 0.0)
            qdec_ref[h] = jnp.exp((pos + 1.0) * lg)
            kdec_ref[h] = jnp.exp((lc - 1.0 - pos) * lg)

    @pl.when(c == 0)
    def _():
        state_ref[...] = jnp.zeros_like(state_ref)

    ang = pos_ref[...] * invf_ref[...]
    cs = jnp.cos(ang)
    sn = jnp.sin(ang)
    cc = jnp.concatenate([cs, cs], axis=-1)
    ss = jnp.concatenate([-sn, sn], axis=-1)
    half = RET_DK // 2
    for h in range(RET_HEADS):
        sl = slice(h * RET_DK, (h + 1) * RET_DK)
        qh = q_ref[:, sl]
        kh = k_ref[:, sl]
        qr = qh * cc + pltpu.roll(qh, half, 1) * ss
        kr = (kh * cc + pltpu.roll(kh, half, 1) * ss) * (RET_DK ** -0.5)
        vb = v_ref[:, sl].astype(BF16)
        scores = lax.dot_general(qr.astype(BF16), kr.astype(BF16), (((1,), (1,)), ((), ())),
                                 preferred_element_type=F32) * dmat_ref[h]
        inner = jnp.dot(scores.astype(BF16), vb, preferred_element_type=F32)
        st = state_ref[h]
        cross = jnp.dot((qr * qdec_ref[h]).astype(BF16), st.astype(BF16), preferred_element_type=F32)
        kdt = (kr * kdec_ref[h]).T.astype(BF16)
        state_ref[h] = st * math.exp(lc * _ret_log_gamma(h)) + jnp.dot(kdt, vb, preferred_element_type=F32)
        o = inner + cross
        o = o * lax.rsqrt(jnp.mean(o * o, axis=-1, keepdims=True) + EPS)
        gh = g_ref[:, sl]
        o_ref[:, sl] = o * (gh * _sigmoid(gh))


def _retention(z3, posf, inv_freq):
    bsz, seq, _ = z3.shape
    lc = 256
    blk = lambda k: pl.BlockSpec((None, lc, RET_QK_WIDTH), lambda b, c, k=k: (b, c, k))
    return pl.pallas_call(
        functools.partial(_retention_kernel, lc=lc),
        grid=(bsz, seq // lc),
        in_specs=[blk(COL_Q // 512), blk(COL_K // 512), blk(COL_V // 512), blk(COL_G // 512),
                  pl.BlockSpec((None, lc, 1), lambda b, c: (b, c, 0)),
                  pl.BlockSpec((1, RET_DK // 2), lambda b, c: (0, 0))],
        out_specs=pl.BlockSpec((None, lc, RET_V_WIDTH), lambda b, c: (b, c, 0)),
        out_shape=jax.ShapeDtypeStruct((bsz, seq, RET_V_WIDTH), F32),
        scratch_shapes=[pltpu.VMEM((RET_HEADS, RET_DK, RET_DV), F32),
                        pltpu.VMEM((RET_HEADS, lc, lc), F32),
                        pltpu.VMEM((RET_HEADS, lc, RET_DK), F32),
                        pltpu.VMEM((RET_HEADS, lc, RET_DK), F32)],
        compiler_params=_cparams(2),
    )(z3, z3, z3, z3, posf, inv_freq)


S5_SCAN_COLS = 512


def _s5_kernel(u_ref, bmat_ref, lam_ref, cmat_ref, d_ref, wglu_ref, o_ref, x_ref, state_ref, *, ts, bsz):
    @pl.when(pl.program_id(0) == 0)
    def _():
        state_ref[...] = jnp.zeros_like(state_ref)

    u = u_ref[...].reshape(ts * bsz, S5_WIDTH)
    x_ref[...] = jnp.dot(u.astype(BF16), bmat_ref[...], preferred_element_type=F32)

    for c0 in range(0, S5_NSTATE, S5_SCAN_COLS):
        re = pl.ds(c0, S5_SCAN_COLS)
        im = pl.ds(S5_NSTATE + c0, S5_SCAN_COLS)
        lr = jnp.broadcast_to(lam_ref[0:1, c0:c0 + S5_SCAN_COLS], (bsz, S5_SCAN_COLS))
        li = jnp.broadcast_to(lam_ref[1:2, c0:c0 + S5_SCAN_COLS], (bsz, S5_SCAN_COLS))

        def step(t, carry):
            xr, xi = carry
            rows = pl.ds(pl.multiple_of(t * bsz, bsz), bsz)
            nr = lr * xr - li * xi + x_ref[rows, re]
            ni = lr * xi + li * xr + x_ref[rows, im]
            x_ref[rows, re] = nr
            x_ref[rows, im] = ni
            return nr, ni

        xr, xi = lax.fori_loop(0, ts, step, (state_ref[:, re], state_ref[:, im]), unroll=8)
        state_ref[:, re] = xr
        state_ref[:, im] = xi

    y = jnp.dot(x_ref[...].astype(BF16), cmat_ref[...], preferred_element_type=F32) + d_ref[...] * u
    y = _gelu(y)
    lg = jnp.dot(y.astype(BF16), wglu_ref[...], preferred_element_type=F32)
    out = lg[:, :S5_WIDTH] * _sigmoid(lg[:, S5_WIDTH:])
    o_ref[...] = out.reshape(ts, bsz, S5_WIDTH)


def _s5(u_sb, bmat, lam, cmat, d, wglu_bf):
    seq, bsz, _ = u_sb.shape
    ts = 128
    full = lambda a: pl.BlockSpec(a.shape, lambda i: (0,) * a.ndim)
    return pl.pallas_call(
        functools.partial(_s5_kernel, ts=ts, bsz=bsz),
        grid=(seq // ts,),
        in_specs=[pl.BlockSpec((ts, bsz, S5_WIDTH), lambda i: (i, 0, 0)),
                  full(bmat), full(lam), full(cmat), full(d), full(wglu_bf)],
        out_specs=pl.BlockSpec((ts, bsz, S5_WIDTH), lambda i: (i, 0, 0)),
        out_shape=jax.ShapeDtypeStruct((seq, bsz, S5_WIDTH), F32),
        scratch_shapes=[pltpu.VMEM((ts * bsz, 2 * S5_NSTATE), F32),
                        pltpu.VMEM((bsz, 2 * S5_NSTATE), F32)],
        compiler_params=_cparams(1),
    )(u_sb, bmat, lam, cmat, d, wglu_bf)


def _memkv_kernel(m_ref, g_ref, w_ref, o_ref):
    mn = _rms(m_ref[...], g_ref[...]).astype(BF16)
    o_ref[...] = jnp.dot(mn, w_ref[...], preferred_element_type=F32)


def _memkv(mem, gain, w_bf):
    bsz, m, d = mem.shape
    n = w_bf.shape[1]
    return pl.pallas_call(
        _memkv_kernel,
        grid=(bsz,),
        in_specs=[pl.BlockSpec((None, m, d), lambda b: (b, 0, 0)),
                  pl.BlockSpec((1, d), lambda b: (0, 0)),
                  pl.BlockSpec((d, n), lambda b: (0, 0))],
        out_specs=pl.BlockSpec((None, m, n), lambda b: (b, 0, 0)),
        out_shape=jax.ShapeDtypeStruct((bsz, m, n), F32),
        compiler_params=_cparams(1),
    )(mem, gain, w_bf)


def _merge_kernel(x_ref, ret_ref, ssm_ref, qm_ref, g0, g1, g2, g3, g4, g5, kv_ref,
                  wr_ref, ws_ref, wm_ref, wo_ref, o_ref):
    kv = kv_ref[...]
    q = qm_ref[...]
    heads = []
    for h in range(MEM_HEADS):
        sl = slice(h * MEM_HEAD_DIM, (h + 1) * MEM_HEAD_DIM)
        kh = kv[:, sl].astype(BF16)
        vh = kv[:, MEM_WIDTH + h * MEM_HEAD_DIM:MEM_WIDTH + (h + 1) * MEM_HEAD_DIM].astype(BF16)
        s = lax.dot_general(q[:, sl].astype(BF16), kh, (((1,), (1,)), ((), ())),
                            preferred_element_type=F32) * (MEM_HEAD_DIM ** -0.5)
        s = s - jnp.max(s, axis=-1, keepdims=True)
        p = jnp.exp(s)
        p = p / jnp.sum(p, axis=-1, keepdims=True)
        heads.append(jnp.dot(p.astype(BF16), vh, preferred_element_type=F32))
    mattn = jnp.concatenate(heads, axis=-1)

    pr = jnp.dot(ret_ref[...].astype(BF16), wr_ref[...], preferred_element_type=F32)
    ps = jnp.dot(ssm_ref[...].astype(BF16), ws_ref[...], preferred_element_type=F32)
    pm = jnp.dot(mattn.astype(BF16), wm_ref[...], preferred_element_type=F32)
    gr = jnp.concatenate([g0[...], g1[...]], axis=-1)
    gs = jnp.concatenate([g2[...], g3[...]], axis=-1)
    gm = jnp.concatenate([g4[...], g5[...]], axis=-1)
    merged = _sigmoid(gr) * pr + _sigmoid(gs) * ps + _sigmoid(gm) * pm
    o_ref[...] = x_ref[...] + jnp.dot(merged.astype(BF16), wo_ref[...], preferred_element_type=F32)


def _merge(x, ret, ssm, z3, kv, wr, ws, wm, wo):
    bsz, seq, d = x.shape
    tm = 512
    gw = 512
    gate = lambda k: pl.BlockSpec((None, tm, gw), lambda b, i, k=k: (b, i, COL_GATES // gw + k))
    full = lambda a: pl.BlockSpec(a.shape, lambda b, i: (0,) * a.ndim)
    return pl.pallas_call(
        _merge_kernel,
        grid=(bsz, seq // tm),
        in_specs=[pl.BlockSpec((None, tm, d), lambda b, i: (b, i, 0)),
                  pl.BlockSpec((None, tm, RET_V_WIDTH), lambda b, i: (b, i, 0)),
                  pl.BlockSpec((None, tm, S5_WIDTH), lambda b, i: (b, i, 0)),
                  pl.BlockSpec((None, tm, MEM_WIDTH), lambda b, i: (b, i, COL_QM // MEM_WIDTH)),
                  gate(0), gate(1), gate(2), gate(3), gate(4), gate(5),
                  pl.BlockSpec((None,) + kv.shape[1:], lambda b, i: (b, 0, 0)),
                  full(wr), full(ws), full(wm), full(wo)],
        out_specs=pl.BlockSpec((None, tm, d), lambda b, i: (b, i, 0)),
        out_shape=jax.ShapeDtypeStruct((bsz, seq, d), F32),
        compiler_params=_cparams(2),
    )(x, ret, ssm, z3, z3, z3, z3, z3, z3, z3, kv, wr, ws, wm, wo)


NEG_INF = float("-inf")
PEER_CAND = [(i, j) for i in range(PEER_TOPK) for j in range(PEER_TOPK) if (i + 1) * (j + 1) <= PEER_TOPK]


def _peer_route_chunk(lanes, s1t_ref, s2_ref, c1t_ref, e2_ref, tau_ref, a_ref, b_ref):
    k = PEER_TOPK
    s = s1t_ref[:, :, lanes]
    for r in range(k):
        m = jnp.max(s, axis=0)
        a_ref[r, :, lanes] = m
        if r + 1 < k:
            s = jnp.where(s == m[None], NEG_INF, s)
    for h in range(PEER_HEADS):
        s = s2_ref[h, :, lanes]
        for r in range(k):
            m = jnp.max(s, axis=0, keepdims=True)
            b_ref[r, pl.ds(h, 1), lanes] = m
            if r + 1 < k:
                s = jnp.where(s == m, NEG_INF, s)
    a = [a_ref[r, :, lanes] for r in range(k)]
    b = [b_ref[r, :, lanes] for r in range(k)]
    ea = [jnp.exp(v - a[0]) for v in a]
    eb = [jnp.exp(v - b[0]) for v in b]
    cand = [a[i] + b[j] for i, j in PEER_CAND]
    work = cand
    for r in range(k):
        tau = functools.reduce(jnp.maximum, work)
        if r + 1 < k:
            work = [jnp.where(w == tau, NEG_INF, w) for w in work]
    zsum = functools.reduce(
        jnp.add, [jnp.where(c >= tau, ea[i] * eb[j], 0.0) for c, (i, j) in zip(cand, PEER_CAND)])
    tau_ref[:, lanes] = tau
    c1t_ref[:, :, lanes] = jnp.exp(s1t_ref[:, :, lanes] - a[0][None]) / zsum[None]
    for h in range(PEER_HEADS):
        e2_ref[h, :, lanes] = jnp.exp(s2_ref[h, :, lanes] - b[0][h:h + 1, :])


def _peer_kernel(h1_ref, gf_ref, wq1t_ref, wq2t_ref, kperm_ref, keys2_ref, u_ref, vt_ref, gfin_ref, y_ref,
                 xnt_ref, s1t_ref, c1t_ref, s2_ref, e2_ref, tau_ref, a_ref, b_ref, hid_ref, at_ref, acc_ref,
                 *, tb, eb):
    e = pl.program_id(1)
    n_e = pl.num_programs(1)
    n_tc = tb // LANES
    rows_per_cell = eb // PEER_NKEYS

    @pl.when(e == 0)
    def _():
        xn = _rms(h1_ref[...], gf_ref[...])
        xnt_ref[...] = xn.T.astype(BF16)
        q1t = jnp.dot(wq1t_ref[...], xnt_ref[...], preferred_element_type=F32).astype(BF16)
        q2t = jnp.dot(wq2t_ref[...], xnt_ref[...], preferred_element_type=F32).astype(BF16)
        s1t = jnp.dot(kperm_ref[...], q1t, preferred_element_type=F32)
        s1t_ref[...] = s1t.reshape(PEER_NKEYS, PEER_HEADS, tb)
        for h in range(PEER_HEADS):
            s2_ref[h] = jnp.dot(keys2_ref[h], q2t[h * PEER_HALF:(h + 1) * PEER_HALF],
                                preferred_element_type=F32)
        for tc in range(n_tc):
            _peer_route_chunk(pl.ds(tc * LANES, LANES), s1t_ref, s2_ref, c1t_ref, e2_ref, tau_ref, a_ref, b_ref)
        acc_ref[...] = jnp.zeros_like(acc_ref)

    hid_ref[...] = jnp.dot(u_ref[...], xnt_ref[...], preferred_element_type=F32)

    def row_body(r, carry):
        i1 = e * rows_per_cell + r
        rows = pl.ds(pl.multiple_of(r * PEER_NKEYS, PEER_NKEYS), PEER_NKEYS)
        for tc in range(n_tc):
            lanes = pl.ds(tc * LANES, LANES)
            w = jnp.zeros((PEER_NKEYS, LANES), F32)
            for h in range(PEER_HEADS):
                s1row = s1t_ref[i1, pl.ds(h, 1), lanes]
                c1row = c1t_ref[i1, pl.ds(h, 1), lanes]
                sel = (s2_ref[h, :, lanes] + s1row) >= tau_ref[pl.ds(h, 1), lanes]
                w = w + jnp.where(sel, e2_ref[h, :, lanes] * c1row, 0.0)
            at_ref[rows, lanes] = (_gelu(hid_ref[rows, lanes]) * w).astype(BF16)
        return carry

    lax.fori_loop(0, rows_per_cell, row_body, 0)
    acc_ref[...] += jnp.dot(vt_ref[...], at_ref[...], preferred_element_type=F32)

    @pl.when(e == n_e - 1)
    def _():
        h2 = h1_ref[...] + acc_ref[...].T
        y_ref[...] = _rms(h2, gfin_ref[...])


def _peer(h1, g_ffn, wq1t_bf, wq2t_bf, kperm_bf, keys2_bf, u_bf, vt_bf, g_final):
    t, d = h1.shape
    n_exp = u_bf.shape[0]
    tb, eb = 512, 512
    full = lambda a: pl.BlockSpec(a.shape, lambda i, e: (0,) * a.ndim)
    return pl.pallas_call(
        functools.partial(_peer_kernel, tb=tb, eb=eb),
        grid=(t // tb, n_exp // eb),
        in_specs=[pl.BlockSpec((tb, d), lambda i, e: (i, 0)),
                  full(g_ffn), full(wq1t_bf), full(wq2t_bf), full(kperm_bf), full(keys2_bf),
                  pl.BlockSpec((eb, d), lambda i, e: (e, 0)),
                  pl.BlockSpec((d, eb), lambda i, e: (0, e)),
                  full(g_final)],
        out_specs=pl.BlockSpec((tb, d), lambda i, e: (i, 0)),
        out_shape=jax.ShapeDtypeStruct((t, d), F32),
        scratch_shapes=[pltpu.VMEM((d, tb), BF16),
                        pltpu.VMEM((PEER_NKEYS, PEER_HEADS, tb), F32),
                        pltpu.VMEM((PEER_NKEYS, PEER_HEADS, tb), F32),
                        pltpu.VMEM((PEER_HEADS, PEER_NKEYS, tb), F32),
                        pltpu.VMEM((PEER_HEADS, PEER_NKEYS, tb), F32),
                        pltpu.VMEM((PEER_HEADS, tb), F32),
                        pltpu.VMEM((PEER_TOPK, PEER_HEADS, tb), F32),
                        pltpu.VMEM((PEER_TOPK, PEER_HEADS, tb), F32),
                        pltpu.VMEM((eb, tb), F32),
                        pltpu.VMEM((eb, tb), BF16),
                        pltpu.VMEM((d, tb), F32)],
        compiler_params=pltpu.CompilerParams(dimension_semantics=("arbitrary", "arbitrary"),
                                             vmem_limit_bytes=PEER_VMEM_LIMIT),
    )(h1, g_ffn, wq1t_bf, wq2t_bf, kperm_bf, keys2_bf, u_bf, vt_bf, g_final)


def _block_diag(blocks):
    g, r, c = blocks.shape
    eye = jnp.eye(g, dtype=blocks.dtype)
    return jnp.einsum("grc,gk->grkc", blocks, eye).reshape(g * r, g * c)


def _peer_params(w_q, keys):
    d = w_q.shape[0]
    wq = w_q.reshape(d, PEER_HEADS, 2, PEER_HALF)
    wq1t = wq[:, :, 0, :].reshape(d, PEER_HEADS * PEER_HALF).T
    wq2t = wq[:, :, 1, :].reshape(d, PEER_HEADS * PEER_HALF).T
    eye = jnp.eye(PEER_HEADS, dtype=keys.dtype)
    kperm = jnp.einsum("hid,hk->ihkd", keys[:, 0], eye).reshape(PEER_NKEYS * PEER_HEADS, PEER_HEADS * PEER_HALF)
    return wq1t.astype(BF16), wq2t.astype(BF16), kperm.astype(BF16), keys[:, 1].astype(BF16)


def kernel(x, mem, positions, norm_mix, w_in, norm_mem, w_mem_kv, s5_a_re, s5_a_im, s5_log_dt, s5_b_re, s5_b_im, s5_c_re, s5_c_im, s5_d, s5_w_glu, w_branch_ret, w_branch_s5, w_branch_mem, w_out, norm_ffn, peer_w_q, peer_keys, peer_u, peer_v, norm_final):
    bsz, seq, d = x.shape
    depth = w_in.shape[0]
    assert depth == 1, "the final RMSNorm is fused into the PEER call of a single layer"
    assert bsz % SUBLANES == 0 and seq % 512 == 0 and d == 1024
    half = RET_DK // 2
    inv_freq = (ROPE_BASE ** (-jnp.arange(half, dtype=F32) / half)).reshape(1, half)
    posf = positions.astype(F32).reshape(bsz, seq, 1)
    row = lambda v: v.reshape(1, -1).astype(F32)

    h = x
    for l in range(depth):
        lam_re, lam_im, bb_re, bb_im = _s5_discretize(s5_a_re[l], s5_a_im[l], s5_log_dt[l], s5_b_re[l], s5_b_im[l])
        bmat = jnp.concatenate([_block_diag(bb_re), _block_diag(bb_im)], axis=1).astype(BF16)
        lam = jnp.stack([lam_re.reshape(-1), lam_im.reshape(-1)])
        c_re_t = jnp.transpose(s5_c_re[l], (0, 2, 1))
        c_im_t = jnp.transpose(s5_c_im[l], (0, 2, 1))
        cmat = jnp.concatenate([_block_diag(c_re_t), -_block_diag(c_im_t)], axis=0).astype(BF16)

        z = _inproj(h.reshape(bsz * seq, d), row(norm_mix[l]), w_in[l].astype(BF16))
        z3 = z.reshape(bsz, seq, IN_COLS)
        ret = _retention(z3, posf, inv_freq)
        u_sb = jnp.transpose(z3[:, :, COL_U:COL_U + S5_WIDTH], (1, 0, 2))
        ssm_sb = _s5(u_sb, bmat, lam, cmat, row(s5_d[l]), s5_w_glu[l].astype(BF16))
        ssm = jnp.transpose(ssm_sb, (1, 0, 2))
        kv = _memkv(mem, row(norm_mem[l]), w_mem_kv[l].astype(BF16))
        h1 = _merge(h, ret, ssm, z3, kv, w_branch_ret[l].astype(BF16), w_branch_s5[l].astype(BF16),
                    w_branch_mem[l].astype(BF16), w_out[l].astype(BF16))
        wq1t, wq2t, kperm, keys2 = _peer_params(peer_w_q[l], peer_keys[l])
        y = _peer(h1.reshape(bsz * seq, d), row(norm_ffn[l]), wq1t, wq2t, kperm, keys2,
                  peer_u[l].astype(BF16), peer_v[l].T.astype(BF16), row(norm_final))
        h = y.reshape(bsz, seq, d)
    return h
```

```python
import functools
import math

import jax
import jax.numpy as jnp
from jax import lax
from jax.experimental import pallas as pl
from jax.experimental.pallas import tpu as pltpu

F32 = jnp.float32
BF16 = jnp.bfloat16

EPS = 1e-6
RET_HEADS = 4
RET_DK = 128
RET_DV = 128
RET_QK_WIDTH = RET_HEADS * RET_DK
RET_V_WIDTH = RET_HEADS * RET_DV
ROPE_BASE = 10000.0
S5_GROUPS = 16
S5_GROUP_DIM = 16
S5_STATE = 64
S5_WIDTH = S5_GROUPS * S5_GROUP_DIM
S5_NSTATE = S5_GROUPS * S5_STATE
MEM_HEADS = 4
MEM_HEAD_DIM = 64
MEM_WIDTH = MEM_HEADS * MEM_HEAD_DIM
PEER_HEADS = 8
PEER_NKEYS = 128
PEER_HALF = 128
PEER_TOPK = 16

LANES = 128
SUBLANES = 8
VMEM_LIMIT = 48 * 1024 * 1024
PEER_VMEM_LIMIT = 56 * 1024 * 1024

COL_Q, COL_K, COL_V, COL_G = 0, 512, 1024, 1536
COL_U, COL_QM, COL_GATES = 2048, 2304, 2560
IN_COLS = 5632


def _cparams(n_axes):
    return pltpu.CompilerParams(dimension_semantics=("arbitrary",) * n_axes, vmem_limit_bytes=VMEM_LIMIT)


def _rms(x, gain):
    return x * lax.rsqrt(jnp.mean(x * x, axis=-1, keepdims=True) + EPS) * gain


def _gelu(x):
    c = math.sqrt(2.0 / math.pi)
    return 0.5 * x * (1.0 + jnp.tanh(c * (x + 0.044715 * (x * x * x))))


def _sigmoid(x):
    return 1.0 / (1.0 + jnp.exp(-x))


def _s5_disc_kernel(are_ref, aim_ref, ldt_ref, bre_ref, bim_ref, lre_ref, lim_ref, bbre_ref, bbim_ref):
    a_re = are_ref[...]
    a_im = aim_ref[...]
    dt = jnp.exp(ldt_ref[...])
    mag = jnp.exp(a_re * dt)
    lam_re = mag * jnp.cos(a_im * dt)
    lam_im = mag * jnp.sin(a_im * dt)
    den = a_re * a_re + a_im * a_im
    nr = lam_re - 1.0
    ni = lam_im
    cr = (nr * a_re + ni * a_im) / den
    ci = (ni * a_re - nr * a_im) / den
    b_re = bre_ref[...]
    b_im = bim_ref[...]
    lre_ref[...] = lam_re
    lim_ref[...] = lam_im
    bbre_ref[...] = cr * b_re - ci * b_im
    bbim_ref[...] = cr * b_im + ci * b_re


def _s5_discretize(a_re, a_im, log_dt, b_re, b_im):
    g, p = a_re.shape
    h = b_re.shape[-1]
    rep = lambda a: jnp.repeat(a, h, axis=0)
    b_re_t = jnp.transpose(b_re, (0, 2, 1)).reshape(g * h, p)
    b_im_t = jnp.transpose(b_im, (0, 2, 1)).reshape(g * h, p)
    shp = jax.ShapeDtypeStruct((g * h, p), F32)
    lam_re, lam_im, bb_re, bb_im = pl.pallas_call(
        _s5_disc_kernel, out_shape=(shp, shp, shp, shp),
    )(rep(a_re), rep(a_im), rep(log_dt[:, None]), b_re_t, b_im_t)
    return lam_re[::h], lam_im[::h], bb_re.reshape(g, h, p), bb_im.reshape(g, h, p)


def _inproj_kernel(x_ref, g_ref, w_ref, z_ref, xn_ref):
    @pl.when(pl.program_id(1) == 0)
    def _():
        xn_ref[...] = _rms(x_ref[...], g_ref[...]).astype(BF16)

    z_ref[...] = jnp.dot(xn_ref[...], w_ref[...], preferred_element_type=F32)


def _inproj(x2, gain, w_bf):
    t, d = x2.shape
    tm, tn = 1024, 512
    return pl.pallas_call(
        _inproj_kernel,
        grid=(t // tm, IN_COLS // tn),
        in_specs=[pl.BlockSpec((tm, d), lambda i, j: (i, 0)),
                  pl.BlockSpec((1, d), lambda i, j: (0, 0)),
                  pl.BlockSpec((d, tn), lambda i, j: (0, j))],
        out_specs=pl.BlockSpec((tm, tn), lambda i, j: (i, j)),
        out_shape=jax.ShapeDtypeStruct((t, IN_COLS), F32),
        scratch_shapes=[pltpu.VMEM((tm, d), BF16)],
        compiler_params=_cparams(2),
    )(x2, gain, w_bf)


def _ret_log_gamma(h):
    return math.log1p(-(2.0 ** (-5.0 - h)))


def _retention_kernel(q_ref, k_ref, v_ref, g_ref, pos_ref, invf_ref, o_ref,
                      state_ref, dmat_ref, qdec_ref, kdec_ref, *, lc):
    b = pl.program_id(0)
    c = pl.program_id(1)

    @pl.when((b == 0) & (c == 0))
    def _():
        row = lax.broadcasted_iota(jnp.int32, (lc, lc), 0)
        col = lax.broadcasted_iota(jnp.int32, (lc, lc), 1)
        rel = (row - col).astype(F32)
        pos = lax.broadcasted_iota(jnp.int32, (lc, RET_DK), 0).astype(F32)
        for h in range(RET_HEADS):
            lg = _ret_log_gamma(h)
            dmat_ref[h] = jnp.where(rel >= 0.0, jnp.exp(rel * lg), 0.0)
            qdec_ref[h] = jnp.exp((pos + 1.0) * lg)
            kdec_ref[h] = jnp.exp((lc - 1.0 - pos) * lg)

    @pl.when(c == 0)
    def _():
        state_ref[...] = jnp.zeros_like(state_ref)

    ang = pos_ref[...] * invf_ref[...]
    cs = jnp.cos(ang)
    sn = jnp.sin(ang)
    cc = jnp.concatenate([cs, cs], axis=-1)
    ss = jnp.concatenate([-sn, sn], axis=-1)
    half = RET_DK // 2
    for h in range(RET_HEADS):
        sl = slice(h * RET_DK, (h + 1) * RET_DK)
        qh = q_ref[:, sl]
        kh = k_ref[:, sl]
        qr = qh * cc + pltpu.roll(qh, half, 1) * ss
        kr = (kh * cc + pltpu.roll(kh, half, 1) * ss) * (RET_DK ** -0.5)
        vb = v_ref[:, sl].astype(BF16)
        scores = lax.dot_general(qr.astype(BF16), kr.astype(BF16), (((1,), (1,)), ((), ())),
                                 preferred_element_type=F32) * dmat_ref[h]
        inner = jnp.dot(scores.astype(BF16), vb, preferred_element_type=F32)
        st = state_ref[h]
        cross = jnp.dot((qr * qdec_ref[h]).astype(BF16), st.astype(BF16), preferred_element_type=F32)
        kdt = (kr * kdec_ref[h]).T.astype(BF16)
        state_ref[h] = st * math.exp(lc * _ret_log_gamma(h)) + jnp.dot(kdt, vb, preferred_element_type=F32)
        o = inner + cross
        o = o * lax.rsqrt(jnp.mean(o * o, axis=-1, keepdims=True) + EPS)
        gh = g_ref[:, sl]
        o_ref[:, sl] = o * (gh * _sigmoid(gh))


def _retention(z3, posf, inv_freq):
    bsz, seq, _ = z3.shape
    lc = 256
    blk = lambda k: pl.BlockSpec((None, lc, RET_QK_WIDTH), lambda b, c, k=k: (b, c, k))
    return pl.pallas_call(
        functools.partial(_retention_kernel, lc=lc),
        grid=(bsz, seq // lc),
        in_specs=[blk(COL_Q // 512), blk(COL_K // 512), blk(COL_V // 512), blk(COL_G // 512),
                  pl.BlockSpec((None, lc, 1), lambda b, c: (b, c, 0)),
                  pl.BlockSpec((1, RET_DK // 2), lambda b, c: (0, 0))],
        out_specs=pl.BlockSpec((None, lc, RET_V_WIDTH), lambda b, c: (b, c, 0)),
        out_shape=jax.ShapeDtypeStruct((bsz, seq, RET_V_WIDTH), F32),
        scratch_shapes=[pltpu.VMEM((RET_HEADS, RET_DK, RET_DV), F32),
                        pltpu.VMEM((RET_HEADS, lc, lc), F32),
                        pltpu.VMEM((RET_HEADS, lc, RET_DK), F32),
                        pltpu.VMEM((RET_HEADS, lc, RET_DK), F32)],
        compiler_params=_cparams(2),
    )(z3, z3, z3, z3, posf, inv_freq)


S5_SCAN_COLS = 512


def _s5_kernel(u_ref, bmat_ref, lam_ref, cmat_ref, d_ref, wglu_ref, o_ref, x_ref, state_ref, *, ts, bsz):
    @pl.when(pl.program_id(0) == 0)
    def _():
        state_ref[...] = jnp.zeros_like(state_ref)

    u = u_ref[...].reshape(ts * bsz, S5_WIDTH)
    x_ref[...] = jnp.dot(u.astype(BF16), bmat_ref[...], preferred_element_type=F32)

    for c0 in range(0, S5_NSTATE, S5_SCAN_COLS):
        re = pl.ds(c0, S5_SCAN_COLS)
        im = pl.ds(S5_NSTATE + c0, S5_SCAN_COLS)
        lr = jnp.broadcast_to(lam_ref[0:1, c0:c0 + S5_SCAN_COLS], (bsz, S5_SCAN_COLS))
        li = jnp.broadcast_to(lam_ref[1:2, c0:c0 + S5_SCAN_COLS], (bsz, S5_SCAN_COLS))

        def step(t, carry):
            xr, xi = carry
            rows = pl.ds(pl.multiple_of(t * bsz, bsz), bsz)
            nr = lr * xr - li * xi + x_ref[rows, re]
            ni = lr * xi + li * xr + x_ref[rows, im]
            x_ref[rows, re] = nr
            x_ref[rows, im] = ni
            return nr, ni

        xr, xi = lax.fori_loop(0, ts, step, (state_ref[:, re], state_ref[:, im]), unroll=8)
        state_ref[:, re] = xr
        state_ref[:, im] = xi

    y = jnp.dot(x_ref[...].astype(BF16), cmat_ref[...], preferred_element_type=F32) + d_ref[...] * u
    y = _gelu(y)
    lg = jnp.dot(y.astype(BF16), wglu_ref[...], preferred_element_type=F32)
    out = lg[:, :S5_WIDTH] * _sigmoid(lg[:, S5_WIDTH:])
    o_ref[...] = out.reshape(ts, bsz, S5_WIDTH)


def _s5(u_sb, bmat, lam, cmat, d, wglu_bf):
    seq, bsz, _ = u_sb.shape
    ts = 128
    full = lambda a: pl.BlockSpec(a.shape, lambda i: (0,) * a.ndim)
    return pl.pallas_call(
        functools.partial(_s5_kernel, ts=ts, bsz=bsz),
        grid=(seq // ts,),
        in_specs=[pl.BlockSpec((ts, bsz, S5_WIDTH), lambda i: (i, 0, 0)),
                  full(bmat), full(lam), full(cmat), full(d), full(wglu_bf)],
        out_specs=pl.BlockSpec((ts, bsz, S5_WIDTH), lambda i: (i, 0, 0)),
        out_shape=jax.ShapeDtypeStruct((seq, bsz, S5_WIDTH), F32),
        scratch_shapes=[pltpu.VMEM((ts * bsz, 2 * S5_NSTATE), F32),
                        pltpu.VMEM((bsz, 2 * S5_NSTATE), F32)],
        compiler_params=_cparams(1),
    )(u_sb, bmat, lam, cmat, d, wglu_bf)


def _memkv_kernel(m_ref, g_ref, w_ref, o_ref):
    mn = _rms(m_ref[...], g_ref[...]).astype(BF16)
    o_ref[...] = jnp.dot(mn, w_ref[...], preferred_element_type=F32)


def _memkv(mem, gain, w_bf):
    bsz, m, d = mem.shape
    n = w_bf.shape[1]
    return pl.pallas_call(
        _memkv_kernel,
        grid=(bsz,),
        in_specs=[pl.BlockSpec((None, m, d), lambda b: (b, 0, 0)),
                  pl.BlockSpec((1, d), lambda b: (0, 0)),
                  pl.BlockSpec((d, n), lambda b: (0, 0))],
        out_specs=pl.BlockSpec((None, m, n), lambda b: (b, 0, 0)),
        out_shape=jax.ShapeDtypeStruct((bsz, m, n), F32),
        compiler_params=_cparams(1),
    )(mem, gain, w_bf)


def _merge_kernel(x_ref, ret_ref, ssm_ref, qm_ref, g0, g1, g2, g3, g4, g5, kv_ref,
                  wr_ref, ws_ref, wm_ref, wo_ref, o_ref):
    kv = kv_ref[...]
    q = qm_ref[...]
    heads = []
    for h in range(MEM_HEADS):
        sl = slice(h * MEM_HEAD_DIM, (h + 1) * MEM_HEAD_DIM)
        kh = kv[:, sl].astype(BF16)
        vh = kv[:, MEM_WIDTH + h * MEM_HEAD_DIM:MEM_WIDTH + (h + 1) * MEM_HEAD_DIM].astype(BF16)
        s = lax.dot_general(q[:, sl].astype(BF16), kh, (((1,), (1,)), ((), ())),
                            preferred_element_type=F32) * (MEM_HEAD_DIM ** -0.5)
        s = s - jnp.max(s, axis=-1, keepdims=True)
        p = jnp.exp(s)
        p = p / jnp.sum(p, axis=-1, keepdims=True)
        heads.append(jnp.dot(p.astype(BF16), vh, preferred_element_type=F32))
    mattn = jnp.concatenate(heads, axis=-1)

    pr = jnp.dot(ret_ref[...].astype(BF16), wr_ref[...], preferred_element_type=F32)
    ps = jnp.dot(ssm_ref[...].astype(BF16), ws_ref[...], preferred_element_type=F32)
    pm = jnp.dot(mattn.astype(BF16), wm_ref[...], preferred_element_type=F32)
    gr = jnp.concatenate([g0[...], g1[...]], axis=-1)
    gs = jnp.concatenate([g2[...], g3[...]], axis=-1)
    gm = jnp.concatenate([g4[...], g5[...]], axis=-1)
    merged = _sigmoid(gr) * pr + _sigmoid(gs) * ps + _sigmoid(gm) * pm
    o_ref[...] = x_ref[...] + jnp.dot(merged.astype(BF16), wo_ref[...], preferred_element_type=F32)


def _merge(x, ret, ssm, z3, kv, wr, ws, wm, wo):
    bsz, seq, d = x.shape
    tm = 512
    gw = 512
    gate = lambda k: pl.BlockSpec((None, tm, gw), lambda b, i, k=k: (b, i, COL_GATES // gw + k))
    full = lambda a: pl.BlockSpec(a.shape, lambda b, i: (0,) * a.ndim)
    return pl.pallas_call(
        _merge_kernel,
        grid=(bsz, seq // tm),
        in_specs=[pl.BlockSpec((None, tm, d), lambda b, i: (b, i, 0)),
                  pl.BlockSpec((None, tm, RET_V_WIDTH), lambda b, i: (b, i, 0)),
                  pl.BlockSpec((None, tm, S5_WIDTH), lambda b, i: (b, i, 0)),
                  pl.BlockSpec((None, tm, MEM_WIDTH), lambda b, i: (b, i, COL_QM // MEM_WIDTH)),
                  gate(0), gate(1), gate(2), gate(3), gate(4), gate(5),
                  pl.BlockSpec((None,) + kv.shape[1:], lambda b, i: (b, 0, 0)),
                  full(wr), full(ws), full(wm), full(wo)],
        out_specs=pl.BlockSpec((None, tm, d), lambda b, i: (b, i, 0)),
        out_shape=jax.ShapeDtypeStruct((bsz, seq, d), F32),
        compiler_params=_cparams(2),
    )(x, ret, ssm, z3, z3, z3, z3, z3, z3, z3, kv, wr, ws, wm, wo)


NEG_INF = float("-inf")
PEER_CAND = [(i, j) for i in range(PEER_TOPK) for j in range(PEER_TOPK) if (i + 1) * (j + 1) <= PEER_TOPK]


def _bitonic_merge_desc(v):
    v = list(v)
    j = len(v) // 2
    while j >= 1:
        for i in range(len(v)):
            l = i ^ j
            if l > i:
                v[i], v[l] = jnp.maximum(v[i], v[l]), jnp.minimum(v[i], v[l])
        j //= 2
    return v


def _sort_desc(v):
    v = list(v)
    n = len(v)
    k = 2
    while k <= n:
        j = k // 2
        while j >= 1:
            for i in range(n):
                l = i ^ j
                if l > i:
                    hi, lo = jnp.maximum(v[i], v[l]), jnp.minimum(v[i], v[l])
                    v[i], v[l] = (hi, lo) if (i & k) == 0 else (lo, hi)
            j //= 2
        k *= 2
    return v


def _peer_route_chunk(tc, s1t_ref, s2_ref, c1t_ref, n1t_ref, r2_ref, e2_ref, a_ref, b_ref):
    k = PEER_TOPK
    lanes = pl.ds(pl.multiple_of(tc * LANES, LANES), LANES)
    top = _sort_desc([s1t_ref[i, :, lanes] for i in range(k)])
    for g in range(1, PEER_NKEYS // k):
        grp = _sort_desc([s1t_ref[g * k + i, :, lanes] for i in range(k)])
        top = _bitonic_merge_desc([jnp.maximum(top[i], grp[k - 1 - i]) for i in range(k)])
    for r in range(k):
        a_ref[r, :, lanes] = top[r]
    for h in range(PEER_HEADS):
        s = s2_ref[h, :, lanes]
        rank = jnp.full(s.shape, float(k), F32)
        for r in range(k):
            m = jnp.max(s, axis=0, keepdims=True)
            b_ref[r, pl.ds(h, 1), lanes] = m
            hit = s == m
            rank = jnp.where(hit, float(r), rank)
            s = jnp.where(hit, NEG_INF, s)
        r2_ref[h, tc] = rank.astype(BF16)
    a = [a_ref[r, :, lanes] for r in range(k)]
    b = [b_ref[r, :, lanes] for r in range(k)]
    ea = [jnp.exp(v - a[0]) for v in a]
    eb = [jnp.exp(v - b[0]) for v in b]
    cand = [a[i] + b[j] for i, j in PEER_CAND]
    work = cand
    for r in range(k):
        tau = functools.reduce(jnp.maximum, work)
        if r + 1 < k:
            work = [jnp.where(w == tau, NEG_INF, w) for w in work]
    zsum = functools.reduce(
        jnp.add, [jnp.where(c >= tau, ea[i] * eb[j], 0.0) for c, (i, j) in zip(cand, PEER_CAND)])
    cnt = [functools.reduce(jnp.add, [jnp.where(a[i] + b[j] >= tau, 1.0, 0.0) for j in range(k)]) for i in range(k)]
    inv_z = 1.0 / zsum

    def group(g, carry):
        rows = pl.ds(pl.multiple_of(g * SUBLANES, SUBLANES), SUBLANES)
        s1 = s1t_ref[rows, :, lanes]
        n1 = jnp.zeros(s1.shape, F32)
        for i in range(k):
            n1 = jnp.where(s1 == a[i][None], cnt[i][None], n1)
        n1t_ref[rows, :, lanes] = n1
        c1t_ref[rows, :, lanes] = jnp.exp(s1 - a[0][None]) * inv_z[None]
        return carry

    lax.fori_loop(0, PEER_NKEYS // SUBLANES, group, 0)
    for h in range(PEER_HEADS):
        e2_ref[h, tc] = jnp.exp(s2_ref[h, :, lanes] - b[0][h:h + 1, :]).astype(BF16)


BF16_ROWS = 16


def _gelu_bf16(x):
    c = math.sqrt(2.0 / math.pi)
    u = x * (c + (c * 0.044715) * (x * x))
    return (0.5 * x) * (1.0 + jnp.tanh(u))


def _peer_elementwise(hid_ref, at_ref, n1s_ref, c1s_ref, r2_ref, e2_ref, *, tb, eb):
    n_slab = PEER_NKEYS // BF16_ROWS
    for r in range(eb // PEER_NKEYS):
        for tc in range(tb // LANES):
            lanes = pl.ds(tc * LANES, LANES)
            w = [None] * n_slab
            for h in range(PEER_HEADS):
                nb = jnp.broadcast_to(n1s_ref[r, pl.ds(h, 1), lanes], (BF16_ROWS, LANES)).astype(BF16)
                cb = jnp.broadcast_to(c1s_ref[r, pl.ds(h, 1), lanes], (BF16_ROWS, LANES)).astype(BF16)
                for k in range(n_slab):
                    rows = pl.ds(k * BF16_ROWS, BF16_ROWS)
                    contrib = jnp.where(r2_ref[h, tc, rows, :] < nb, e2_ref[h, tc, rows, :] * cb, 0)
                    w[k] = contrib if w[k] is None else w[k] + contrib
            for k in range(n_slab):
                rows = pl.ds(r * PEER_NKEYS + k * BF16_ROWS, BF16_ROWS)
                at_ref[tc, rows, :] = _gelu_bf16(hid_ref[tc, rows, :]) * w[k]


def _peer_kernel(h1_ref, gf_ref, wq1t_ref, wq2t_ref, kperm_ref, keys2_ref, u_ref, vt_ref, gfin_ref, y_ref,
                 xnt_ref, s1t_ref, c1t_ref, n1t_ref, s2_ref, r2_ref, e2_ref, a_ref, b_ref,
                 hid_ref, at_ref, acc_ref, n1s_ref, c1s_ref, *, tb, eb):
    s = pl.program_id(1)
    rows_per_cell = eb // PEER_NKEYS
    n_tc = tb // LANES

    @pl.when(s == 0)
    def _():
        xn = _rms(h1_ref[...], gf_ref[...])
        xnt_ref[...] = xn.T.astype(BF16)
        q1t = jnp.dot(wq1t_ref[...], xnt_ref[...], preferred_element_type=F32).astype(BF16)
        q2t = jnp.dot(wq2t_ref[...], xnt_ref[...], preferred_element_type=F32).astype(BF16)
        s1t = jnp.dot(kperm_ref[...], q1t, preferred_element_type=F32)
        s1t_ref[...] = s1t.reshape(PEER_NKEYS, PEER_HEADS, tb)
        for h in range(PEER_HEADS):
            s2_ref[h] = jnp.dot(keys2_ref[h], q2t[h * PEER_HALF:(h + 1) * PEER_HALF],
                                preferred_element_type=F32)
        def route(tc, carry):
            _peer_route_chunk(tc, s1t_ref, s2_ref, c1t_ref, n1t_ref, r2_ref, e2_ref, a_ref, b_ref)
            return carry

        lax.fori_loop(0, tb // LANES, route, 0)
        acc_ref[...] = jnp.zeros_like(acc_ref)

    first = pl.multiple_of(s * rows_per_cell, rows_per_cell)
    n1s_ref[...] = n1t_ref[pl.ds(first, rows_per_cell)]
    c1s_ref[...] = c1t_ref[pl.ds(first, rows_per_cell)]
    hid = jnp.dot(u_ref[...], xnt_ref[...], preferred_element_type=F32).astype(BF16)
    for tc in range(n_tc):
        hid_ref[tc] = hid[:, tc * LANES:(tc + 1) * LANES]
    _peer_elementwise(hid_ref, at_ref, n1s_ref, c1s_ref, r2_ref, e2_ref, tb=tb, eb=eb)
    at_full = jnp.concatenate([at_ref[tc] for tc in range(n_tc)], axis=1)
    acc_ref[...] += jnp.dot(vt_ref[...], at_full, preferred_element_type=F32)

    @pl.when(s == pl.num_programs(1) - 1)
    def _():
        h2 = h1_ref[...] + acc_ref[...].T
        y_ref[...] = _rms(h2, gfin_ref[...])


def _peer(h1, g_ffn, wq1t_bf, wq2t_bf, kperm_bf, keys2_bf, u_bf, vt_bf, g_final):
    t, d = h1.shape
    n_exp = u_bf.shape[0]
    tb, eb = 512, 1024
    full = lambda a: pl.BlockSpec(a.shape, lambda i, s: (0,) * a.ndim)
    route_t = pltpu.VMEM((PEER_NKEYS, PEER_HEADS, tb), F32)
    ranks = pltpu.VMEM((PEER_TOPK, PEER_HEADS, tb), F32)

    def chunked(*lead):
        return pltpu.VMEM(lead[:-1] + (tb // LANES, lead[-1], LANES), BF16)

    return pl.pallas_call(
        functools.partial(_peer_kernel, tb=tb, eb=eb),
        grid=(t // tb, n_exp // eb),
        in_specs=[pl.BlockSpec((tb, d), lambda i, s: (i, 0)),
                  full(g_ffn), full(wq1t_bf), full(wq2t_bf), full(kperm_bf), full(keys2_bf),
                  pl.BlockSpec((eb, d), lambda i, s: (s, 0)),
                  pl.BlockSpec((d, eb), lambda i, s: (0, s)),
                  full(g_final)],
        out_specs=pl.BlockSpec((tb, d), lambda i, s: (i, 0)),
        out_shape=jax.ShapeDtypeStruct((t, d), F32),
        scratch_shapes=[pltpu.VMEM((d, tb), BF16),
                        route_t,
                        route_t,
                        route_t,
                        pltpu.VMEM((PEER_HEADS, PEER_NKEYS, tb), F32),
                        chunked(PEER_HEADS, PEER_NKEYS),
                        chunked(PEER_HEADS, PEER_NKEYS),
                        ranks, ranks,
                        chunked(eb),
                        chunked(eb),
                        pltpu.VMEM((d, tb), F32),
                        pltpu.VMEM((eb // PEER_NKEYS, PEER_HEADS, tb), F32),
                        pltpu.VMEM((eb // PEER_NKEYS, PEER_HEADS, tb), F32)],
        compiler_params=pltpu.CompilerParams(dimension_semantics=("arbitrary", "arbitrary"),
                                             vmem_limit_bytes=PEER_VMEM_LIMIT),
    )(h1, g_ffn, wq1t_bf, wq2t_bf, kperm_bf, keys2_bf, u_bf, vt_bf, g_final)


def _block_diag(blocks):
    g, r, c = blocks.shape
    eye = jnp.eye(g, dtype=blocks.dtype)
    return jnp.einsum("grc,gk->grkc", blocks, eye).reshape(g * r, g * c)


def _peer_params(w_q, keys):
    d = w_q.shape[0]
    wq = w_q.reshape(d, PEER_HEADS, 2, PEER_HALF)
    wq1t = wq[:, :, 0, :].reshape(d, PEER_HEADS * PEER_HALF).T
    wq2t = wq[:, :, 1, :].reshape(d, PEER_HEADS * PEER_HALF).T
    eye = jnp.eye(PEER_HEADS, dtype=keys.dtype)
    kperm = jnp.einsum("hid,hk->ihkd", keys[:, 0], eye).reshape(PEER_NKEYS * PEER_HEADS, PEER_HEADS * PEER_HALF)
    return wq1t.astype(BF16), wq2t.astype(BF16), kperm.astype(BF16), keys[:, 1].astype(BF16)


def kernel(x, mem, positions, norm_mix, w_in, norm_mem, w_mem_kv, s5_a_re, s5_a_im, s5_log_dt, s5_b_re, s5_b_im, s5_c_re, s5_c_im, s5_d, s5_w_glu, w_branch_ret, w_branch_s5, w_branch_mem, w_out, norm_ffn, peer_w_q, peer_keys, peer_u, peer_v, norm_final):
    bsz, seq, d = x.shape
    depth = w_in.shape[0]
    assert depth == 1, "the final RMSNorm is fused into the PEER call of a single layer"
    assert bsz % SUBLANES == 0 and seq % 512 == 0 and d == 1024
    half = RET_DK // 2
    inv_freq = (ROPE_BASE ** (-jnp.arange(half, dtype=F32) / half)).reshape(1, half)
    posf = positions.astype(F32).reshape(bsz, seq, 1)
    row = lambda v: v.reshape(1, -1).astype(F32)

    h = x
    for l in range(depth):
        lam_re, lam_im, bb_re, bb_im = _s5_discretize(s5_a_re[l], s5_a_im[l], s5_log_dt[l], s5_b_re[l], s5_b_im[l])
        bmat = jnp.concatenate([_block_diag(bb_re), _block_diag(bb_im)], axis=1).astype(BF16)
        lam = jnp.stack([lam_re.reshape(-1), lam_im.reshape(-1)])
        c_re_t = jnp.transpose(s5_c_re[l], (0, 2, 1))
        c_im_t = jnp.transpose(s5_c_im[l], (0, 2, 1))
        cmat = jnp.concatenate([_block_diag(c_re_t), -_block_diag(c_im_t)], axis=0).astype(BF16)

        z = _inproj(h.reshape(bsz * seq, d), row(norm_mix[l]), w_in[l].astype(BF16))
        z3 = z.reshape(bsz, seq, IN_COLS)
        ret = _retention(z3, posf, inv_freq)
        u_sb = jnp.transpose(z3[:, :, COL_U:COL_U + S5_WIDTH], (1, 0, 2))
        ssm_sb = _s5(u_sb, bmat, lam, cmat, row(s5_d[l]), s5_w_glu[l].astype(BF16))
        ssm = jnp.transpose(ssm_sb, (1, 0, 2))
        kv = _memkv(mem, row(norm_mem[l]), w_mem_kv[l].astype(BF16))
        h1 = _merge(h, ret, ssm, z3, kv, w_branch_ret[l].astype(BF16), w_branch_s5[l].astype(BF16),
                    w_branch_mem[l].astype(BF16), w_out[l].astype(BF16))
        wq1t, wq2t, kperm, keys2 = _peer_params(peer_w_q[l], peer_keys[l])
        y = _peer(h1.reshape(bsz * seq, d), row(norm_ffn[l]), wq1t, wq2t, kperm, keys2,
                  peer_u[l].astype(BF16), peer_v[l].T.astype(BF16), row(norm_final))
        h = y.reshape(bsz, seq, d)
    return h
```

```python
import functools
import math

import jax
import jax.numpy as jnp
from jax import lax
from jax.experimental import pallas as pl
from jax.experimental.pallas import tpu as pltpu

F32 = jnp.float32
BF16 = jnp.bfloat16

EPS = 1e-6
RET_HEADS = 4
RET_DK = 128
RET_DV = 128
RET_QK_WIDTH = RET_HEADS * RET_DK
RET_V_WIDTH = RET_HEADS * RET_DV
ROPE_BASE = 10000.0
S5_GROUPS = 16
S5_GROUP_DIM = 16
S5_STATE = 64
S5_WIDTH = S5_GROUPS * S5_GROUP_DIM
S5_NSTATE = S5_GROUPS * S5_STATE
MEM_HEADS = 4
MEM_HEAD_DIM = 64
MEM_WIDTH = MEM_HEADS * MEM_HEAD_DIM
PEER_HEADS = 8
PEER_NKEYS = 128
PEER_HALF = 128
PEER_TOPK = 16

LANES = 128
SUBLANES = 8
VMEM_LIMIT = 48 * 1024 * 1024
PEER_VMEM_LIMIT = 56 * 1024 * 1024

COL_Q, COL_K, COL_V, COL_G = 0, 512, 1024, 1536
COL_U, COL_QM, COL_GATES = 2048, 2304, 2560
IN_COLS = 5632


def _cparams(n_axes):
    return pltpu.CompilerParams(dimension_semantics=("arbitrary",) * n_axes, vmem_limit_bytes=VMEM_LIMIT)


def _rms(x, gain):
    return x * lax.rsqrt(jnp.mean(x * x, axis=-1, keepdims=True) + EPS) * gain


def _gelu(x):
    c = math.sqrt(2.0 / math.pi)
    return 0.5 * x * (1.0 + jnp.tanh(c * (x + 0.044715 * (x * x * x))))


def _sigmoid(x):
    return 1.0 / (1.0 + jnp.exp(-x))


def _s5_disc_kernel(are_ref, aim_ref, ldt_ref, bre_ref, bim_ref, lre_ref, lim_ref, bbre_ref, bbim_ref):
    a_re = are_ref[...]
    a_im = aim_ref[...]
    dt = jnp.exp(ldt_ref[...])
    mag = jnp.exp(a_re * dt)
    lam_re = mag * jnp.cos(a_im * dt)
    lam_im = mag * jnp.sin(a_im * dt)
    den = a_re * a_re + a_im * a_im
    nr = lam_re - 1.0
    ni = lam_im
    cr = (nr * a_re + ni * a_im) / den
    ci = (ni * a_re - nr * a_im) / den
    b_re = bre_ref[...]
    b_im = bim_ref[...]
    lre_ref[...] = lam_re
    lim_ref[...] = lam_im
    bbre_ref[...] = cr * b_re - ci * b_im
    bbim_ref[...] = cr * b_im + ci * b_re


def _s5_discretize(a_re, a_im, log_dt, b_re, b_im):
    g, p = a_re.shape
    h = b_re.shape[-1]
    rep = lambda a: jnp.repeat(a, h, axis=0)
    b_re_t = jnp.transpose(b_re, (0, 2, 1)).reshape(g * h, p)
    b_im_t = jnp.transpose(b_im, (0, 2, 1)).reshape(g * h, p)
    shp = jax.ShapeDtypeStruct((g * h, p), F32)
    lam_re, lam_im, bb_re, bb_im = pl.pallas_call(
        _s5_disc_kernel, out_shape=(shp, shp, shp, shp),
    )(rep(a_re), rep(a_im), rep(log_dt[:, None]), b_re_t, b_im_t)
    return lam_re[::h], lam_im[::h], bb_re.reshape(g, h, p), bb_im.reshape(g, h, p)


def _inproj_kernel(x_ref, g_ref, w_ref, z_ref, xn_ref):
    @pl.when(pl.program_id(1) == 0)
    def _():
        xn_ref[...] = _rms(x_ref[...], g_ref[...]).astype(BF16)

    z_ref[...] = jnp.dot(xn_ref[...], w_ref[...], preferred_element_type=F32)


def _inproj(x2, gain, w_bf):
    t, d = x2.shape
    tm, tn = 1024, 512
    return pl.pallas_call(
        _inproj_kernel,
        grid=(t // tm, IN_COLS // tn),
        in_specs=[pl.BlockSpec((tm, d), lambda i, j: (i, 0)),
                  pl.BlockSpec((1, d), lambda i, j: (0, 0)),
                  pl.BlockSpec((d, tn), lambda i, j: (0, j))],
        out_specs=pl.BlockSpec((tm, tn), lambda i, j: (i, j)),
        out_shape=jax.ShapeDtypeStruct((t, IN_COLS), F32),
        scratch_shapes=[pltpu.VMEM((tm, d), BF16)],
        compiler_params=_cparams(2),
    )(x2, gain, w_bf)


def _ret_log_gamma(h):
    return math.log1p(-(2.0 ** (-5.0 - h)))


def _retention_kernel(q_ref, k_ref, v_ref, g_ref, pos_ref, invf_ref, o_ref,
                      state_ref, dmat_ref, qdec_ref, kdec_ref, *, lc):
    b = pl.program_id(0)
    c = pl.program_id(1)

    @pl.when((b == 0) & (c == 0))
    def _():
        row = lax.broadcasted_iota(jnp.int32, (lc, lc), 0)
        col = lax.broadcasted_iota(jnp.int32, (lc, lc), 1)
        rel = (row - col).astype(F32)
        pos = lax.broadcasted_iota(jnp.int32, (lc, RET_DK), 0).astype(F32)
        for h in range(RET_HEADS):
            lg = _ret_log_gamma(h)
            dmat_ref[h] = jnp.where(rel >= 0.0, jnp.exp(rel * lg), 0.0)
            qdec_ref[h] = jnp.exp((pos + 1.0) * lg)
            kdec_ref[h] = jnp.exp((lc - 1.0 - pos) * lg)

    @pl.when(c == 0)
    def _():
        state_ref[...] = jnp.zeros_like(state_ref)

    ang = pos_ref[...] * invf_ref[...]
    cs = jnp.cos(ang)
    sn = jnp.sin(ang)
    cc = jnp.concatenate([cs, cs], axis=-1)
    ss = jnp.concatenate([-sn, sn], axis=-1)
    half = RET_DK // 2
    for h in range(RET_HEADS):
        sl = slice(h * RET_DK, (h + 1) * RET_DK)
        qh = q_ref[:, sl]
        kh = k_ref[:, sl]
        qr = qh * cc + pltpu.roll(qh, half, 1) * ss
        kr = (kh * cc + pltpu.roll(kh, half, 1) * ss) * (RET_DK ** -0.5)
        vb = v_ref[:, sl].astype(BF16)
        scores = lax.dot_general(qr.astype(BF16), kr.astype(BF16), (((1,), (1,)), ((), ())),
                                 preferred_element_type=F32) * dmat_ref[h]
        inner = jnp.dot(scores.astype(BF16), vb, preferred_element_type=F32)
        st = state_ref[h]
        cross = jnp.dot((qr * qdec_ref[h]).astype(BF16), st.astype(BF16), preferred_element_type=F32)
        kdt = (kr * kdec_ref[h]).T.astype(BF16)
        state_ref[h] = st * math.exp(lc * _ret_log_gamma(h)) + jnp.dot(kdt, vb, preferred_element_type=F32)
        o = inner + cross
        o = o * lax.rsqrt(jnp.mean(o * o, axis=-1, keepdims=True) + EPS)
        gh = g_ref[:, sl]
        o_ref[:, sl] = o * (gh * _sigmoid(gh))


def _retention(z3, posf, inv_freq):
    bsz, seq, _ = z3.shape
    lc = 256
    blk = lambda k: pl.BlockSpec((None, lc, RET_QK_WIDTH), lambda b, c, k=k: (b, c, k))
    return pl.pallas_call(
        functools.partial(_retention_kernel, lc=lc),
        grid=(bsz, seq // lc),
        in_specs=[blk(COL_Q // 512), blk(COL_K // 512), blk(COL_V // 512), blk(COL_G // 512),
                  pl.BlockSpec((None, lc, 1), lambda b, c: (b, c, 0)),
                  pl.BlockSpec((1, RET_DK // 2), lambda b, c: (0, 0))],
        out_specs=pl.BlockSpec((None, lc, RET_V_WIDTH), lambda b, c: (b, c, 0)),
        out_shape=jax.ShapeDtypeStruct((bsz, seq, RET_V_WIDTH), F32),
        scratch_shapes=[pltpu.VMEM((RET_HEADS, RET_DK, RET_DV), F32),
                        pltpu.VMEM((RET_HEADS, lc, lc), F32),
                        pltpu.VMEM((RET_HEADS, lc, RET_DK), F32),
                        pltpu.VMEM((RET_HEADS, lc, RET_DK), F32)],
        compiler_params=_cparams(2),
    )(z3, z3, z3, z3, posf, inv_freq)


S5_SCAN_COLS = 512


def _s5_kernel(u_ref, bmat_ref, lam_ref, cmat_ref, d_ref, wglu_ref, o_ref, x_ref, state_ref, *, ts, bsz):
    @pl.when(pl.program_id(0) == 0)
    def _():
        state_ref[...] = jnp.zeros_like(state_ref)

    u = u_ref[...].reshape(ts * bsz, S5_WIDTH)
    x_ref[...] = jnp.dot(u.astype(BF16), bmat_ref[...], preferred_element_type=F32)

    for c0 in range(0, S5_NSTATE, S5_SCAN_COLS):
        re = pl.ds(c0, S5_SCAN_COLS)
        im = pl.ds(S5_NSTATE + c0, S5_SCAN_COLS)
        lr = jnp.broadcast_to(lam_ref[0:1, c0:c0 + S5_SCAN_COLS], (bsz, S5_SCAN_COLS))
        li = jnp.broadcast_to(lam_ref[1:2, c0:c0 + S5_SCAN_COLS], (bsz, S5_SCAN_COLS))

        def step(t, carry):
            xr, xi = carry
            rows = pl.ds(pl.multiple_of(t * bsz, bsz), bsz)
            nr = lr * xr - li * xi + x_ref[rows, re]
            ni = lr * xi + li * xr + x_ref[rows, im]
            x_ref[rows, re] = nr
            x_ref[rows, im] = ni
            return nr, ni

        xr, xi = lax.fori_loop(0, ts, step, (state_ref[:, re], state_ref[:, im]), unroll=8)
        state_ref[:, re] = xr
        state_ref[:, im] = xi

    y = jnp.dot(x_ref[...].astype(BF16), cmat_ref[...], preferred_element_type=F32) + d_ref[...] * u
    y = _gelu(y)
    lg = jnp.dot(y.astype(BF16), wglu_ref[...], preferred_element_type=F32)
    out = lg[:, :S5_WIDTH] * _sigmoid(lg[:, S5_WIDTH:])
    o_ref[...] = out.reshape(ts, bsz, S5_WIDTH)


def _s5(u_sb, bmat, lam, cmat, d, wglu_bf):
    seq, bsz, _ = u_sb.shape
    ts = 128
    full = lambda a: pl.BlockSpec(a.shape, lambda i: (0,) * a.ndim)
    return pl.pallas_call(
        functools.partial(_s5_kernel, ts=ts, bsz=bsz),
        grid=(seq // ts,),
        in_specs=[pl.BlockSpec((ts, bsz, S5_WIDTH), lambda i: (i, 0, 0)),
                  full(bmat), full(lam), full(cmat), full(d), full(wglu_bf)],
        out_specs=pl.BlockSpec((ts, bsz, S5_WIDTH), lambda i: (i, 0, 0)),
        out_shape=jax.ShapeDtypeStruct((seq, bsz, S5_WIDTH), F32),
        scratch_shapes=[pltpu.VMEM((ts * bsz, 2 * S5_NSTATE), F32),
                        pltpu.VMEM((bsz, 2 * S5_NSTATE), F32)],
        compiler_params=_cparams(1),
    )(u_sb, bmat, lam, cmat, d, wglu_bf)


def _memkv_kernel(m_ref, g_ref, w_ref, o_ref):
    mn = _rms(m_ref[...], g_ref[...]).astype(BF16)
    o_ref[...] = jnp.dot(mn, w_ref[...], preferred_element_type=F32)


def _memkv(mem, gain, w_bf):
    bsz, m, d = mem.shape
    n = w_bf.shape[1]
    return pl.pallas_call(
        _memkv_kernel,
        grid=(bsz,),
        in_specs=[pl.BlockSpec((None, m, d), lambda b: (b, 0, 0)),
                  pl.BlockSpec((1, d), lambda b: (0, 0)),
                  pl.BlockSpec((d, n), lambda b: (0, 0))],
        out_specs=pl.BlockSpec((None, m, n), lambda b: (b, 0, 0)),
        out_shape=jax.ShapeDtypeStruct((bsz, m, n), F32),
        compiler_params=_cparams(1),
    )(mem, gain, w_bf)


def _merge_kernel(x_ref, ret_ref, ssm_ref, qm_ref, g0, g1, g2, g3, g4, g5, kv_ref,
                  wr_ref, ws_ref, wm_ref, wo_ref, o_ref):
    kv = kv_ref[...]
    q = qm_ref[...]
    heads = []
    for h in range(MEM_HEADS):
        sl = slice(h * MEM_HEAD_DIM, (h + 1) * MEM_HEAD_DIM)
        kh = kv[:, sl].astype(BF16)
        vh = kv[:, MEM_WIDTH + h * MEM_HEAD_DIM:MEM_WIDTH + (h + 1) * MEM_HEAD_DIM].astype(BF16)
        s = lax.dot_general(q[:, sl].astype(BF16), kh, (((1,), (1,)), ((), ())),
                            preferred_element_type=F32) * (MEM_HEAD_DIM ** -0.5)
        s = s - jnp.max(s, axis=-1, keepdims=True)
        p = jnp.exp(s)
        p = p / jnp.sum(p, axis=-1, keepdims=True)
        heads.append(jnp.dot(p.astype(BF16), vh, preferred_element_type=F32))
    mattn = jnp.concatenate(heads, axis=-1)

    pr = jnp.dot(ret_ref[...].astype(BF16), wr_ref[...], preferred_element_type=F32)
    ps = jnp.dot(ssm_ref[...].astype(BF16), ws_ref[...], preferred_element_type=F32)
    pm = jnp.dot(mattn.astype(BF16), wm_ref[...], preferred_element_type=F32)
    gr = jnp.concatenate([g0[...], g1[...]], axis=-1)
    gs = jnp.concatenate([g2[...], g3[...]], axis=-1)
    gm = jnp.concatenate([g4[...], g5[...]], axis=-1)
    merged = _sigmoid(gr) * pr + _sigmoid(gs) * ps + _sigmoid(gm) * pm
    o_ref[...] = x_ref[...] + jnp.dot(merged.astype(BF16), wo_ref[...], preferred_element_type=F32)


def _merge(x, ret, ssm, z3, kv, wr, ws, wm, wo):
    bsz, seq, d = x.shape
    tm = 512
    gw = 512
    gate = lambda k: pl.BlockSpec((None, tm, gw), lambda b, i, k=k: (b, i, COL_GATES // gw + k))
    full = lambda a: pl.BlockSpec(a.shape, lambda b, i: (0,) * a.ndim)
    return pl.pallas_call(
        _merge_kernel,
        grid=(bsz, seq // tm),
        in_specs=[pl.BlockSpec((None, tm, d), lambda b, i: (b, i, 0)),
                  pl.BlockSpec((None, tm, RET_V_WIDTH), lambda b, i: (b, i, 0)),
                  pl.BlockSpec((None, tm, S5_WIDTH), lambda b, i: (b, i, 0)),
                  pl.BlockSpec((None, tm, MEM_WIDTH), lambda b, i: (b, i, COL_QM // MEM_WIDTH)),
                  gate(0), gate(1), gate(2), gate(3), gate(4), gate(5),
                  pl.BlockSpec((None,) + kv.shape[1:], lambda b, i: (b, 0, 0)),
                  full(wr), full(ws), full(wm), full(wo)],
        out_specs=pl.BlockSpec((None, tm, d), lambda b, i: (b, i, 0)),
        out_shape=jax.ShapeDtypeStruct((bsz, seq, d), F32),
        compiler_params=_cparams(2),
    )(x, ret, ssm, z3, z3, z3, z3, z3, z3, z3, kv, wr, ws, wm, wo)


NEG_INF = float("-inf")
PEER_CAND = [(i, j) for i in range(PEER_TOPK) for j in range(PEER_TOPK) if (i + 1) * (j + 1) <= PEER_TOPK]


def _bitonic_merge_desc(v):
    v = list(v)
    j = len(v) // 2
    while j >= 1:
        for i in range(len(v)):
            l = i ^ j
            if l > i:
                v[i], v[l] = jnp.maximum(v[i], v[l]), jnp.minimum(v[i], v[l])
        j //= 2
    return v


def _sort_desc(v):
    v = list(v)
    n = len(v)
    k = 2
    while k <= n:
        j = k // 2
        while j >= 1:
            for i in range(n):
                l = i ^ j
                if l > i:
                    hi, lo = jnp.maximum(v[i], v[l]), jnp.minimum(v[i], v[l])
                    v[i], v[l] = (hi, lo) if (i & k) == 0 else (lo, hi)
            j //= 2
        k *= 2
    return v


def _peer_route_chunk(tc, s1t_ref, s2_ref, c1t_ref, n1t_ref, r2_ref, e2_ref, a_ref, b_ref):
    k = PEER_TOPK
    lanes = pl.ds(pl.multiple_of(tc * LANES, LANES), LANES)
    top = _sort_desc([s1t_ref[i, :, lanes] for i in range(k)])
    for g in range(1, PEER_NKEYS // k):
        grp = _sort_desc([s1t_ref[g * k + i, :, lanes] for i in range(k)])
        top = _bitonic_merge_desc([jnp.maximum(top[i], grp[k - 1 - i]) for i in range(k)])
    for r in range(k):
        a_ref[r, :, lanes] = top[r]
    for h in range(PEER_HEADS):
        s = s2_ref[h, :, lanes]
        rank = jnp.full(s.shape, float(k), F32)
        for r in range(k):
            m = jnp.max(s, axis=0, keepdims=True)
            b_ref[r, pl.ds(h, 1), lanes] = m
            hit = s == m
            rank = jnp.where(hit, float(r), rank)
            s = jnp.where(hit, NEG_INF, s)
        r2_ref[h, tc] = rank.astype(BF16)
    a = [a_ref[r, :, lanes] for r in range(k)]
    b = [b_ref[r, :, lanes] for r in range(k)]
    ea = [jnp.exp(v - a[0]) for v in a]
    eb = [jnp.exp(v - b[0]) for v in b]
    cand = [a[i] + b[j] for i, j in PEER_CAND]
    work = cand
    for r in range(k):
        tau = functools.reduce(jnp.maximum, work)
        if r + 1 < k:
            work = [jnp.where(w == tau, NEG_INF, w) for w in work]
    zsum = functools.reduce(
        jnp.add, [jnp.where(c >= tau, ea[i] * eb[j], 0.0) for c, (i, j) in zip(cand, PEER_CAND)])
    cnt = [functools.reduce(jnp.add, [jnp.where(a[i] + b[j] >= tau, 1.0, 0.0) for j in range(k)]) for i in range(k)]
    inv_z = 1.0 / zsum

    def group(g, carry):
        rows = pl.ds(pl.multiple_of(g * SUBLANES, SUBLANES), SUBLANES)
        s1 = s1t_ref[rows, :, lanes]
        n1 = jnp.zeros(s1.shape, F32)
        for i in range(k):
            n1 = jnp.where(s1 == a[i][None], cnt[i][None], n1)
        n1t_ref[rows, :, lanes] = n1
        c1t_ref[rows, :, lanes] = jnp.exp(s1 - a[0][None]) * inv_z[None]
        return carry

    lax.fori_loop(0, PEER_NKEYS // SUBLANES, group, 0)
    for h in range(PEER_HEADS):
        e2_ref[h, tc] = jnp.exp(s2_ref[h, :, lanes] - b[0][h:h + 1, :]).astype(BF16)


BF16_ROWS = 16


def _gelu_bf16(x):
    c = math.sqrt(2.0 / math.pi)
    u = x * (c + (c * 0.044715) * (x * x))
    return (0.5 * x) * (1.0 + jnp.tanh(u))


def _peer_elementwise(hid_ref, at_ref, n1s_ref, c1s_ref, r2_ref, e2_ref, *, tb, eb):
    n_slab = PEER_NKEYS // BF16_ROWS
    for r in range(eb // PEER_NKEYS):
        for tc in range(tb // LANES):
            lanes = pl.ds(tc * LANES, LANES)
            w = [None] * n_slab
            for h in range(PEER_HEADS):
                nb = jnp.broadcast_to(n1s_ref[r, pl.ds(h, 1), lanes], (BF16_ROWS, LANES)).astype(BF16)
                cb = jnp.broadcast_to(c1s_ref[r, pl.ds(h, 1), lanes], (BF16_ROWS, LANES)).astype(BF16)
                for k in range(n_slab):
                    rows = pl.ds(k * BF16_ROWS, BF16_ROWS)
                    contrib = jnp.where(r2_ref[h, tc, rows, :] < nb, e2_ref[h, tc, rows, :] * cb, 0)
                    w[k] = contrib if w[k] is None else w[k] + contrib
            for k in range(n_slab):
                rows = pl.ds(r * PEER_NKEYS + k * BF16_ROWS, BF16_ROWS)
                at_ref[tc, rows, :] = _gelu_bf16(hid_ref[tc, rows, :]) * w[k]


def _peer_kernel(h1_ref, gf_ref, wq1t_ref, wq2t_ref, kperm_ref, keys2_ref, u_ref, vt_ref, gfin_ref, y_ref,
                 xnt_ref, s1t_ref, c1t_ref, n1t_ref, s2_ref, r2_ref, e2_ref, a_ref, b_ref,
                 hid_ref, at_ref, acc_ref, n1s_ref, c1s_ref, *, tb, eb):
    s = pl.program_id(1)
    rows_per_cell = eb // PEER_NKEYS
    n_tc = tb // LANES

    @pl.when(s == 0)
    def _():
        xn = _rms(h1_ref[...], gf_ref[...])
        xnt_ref[...] = xn.T.astype(BF16)
        q1t = jnp.dot(wq1t_ref[...], xnt_ref[...], preferred_element_type=F32).astype(BF16)
        q2t = jnp.dot(wq2t_ref[...], xnt_ref[...], preferred_element_type=F32).astype(BF16)
        s1t = jnp.dot(kperm_ref[...], q1t, preferred_element_type=F32)
        s1t_ref[...] = s1t.reshape(PEER_NKEYS, PEER_HEADS, tb)
        for h in range(PEER_HEADS):
            s2_ref[h] = jnp.dot(keys2_ref[h], q2t[h * PEER_HALF:(h + 1) * PEER_HALF],
                                preferred_element_type=F32)
        def route(tc, carry):
            _peer_route_chunk(tc, s1t_ref, s2_ref, c1t_ref, n1t_ref, r2_ref, e2_ref, a_ref, b_ref)
            return carry

        lax.fori_loop(0, tb // LANES, route, 0)
        acc_ref[...] = jnp.zeros_like(acc_ref)

    first = pl.multiple_of(s * rows_per_cell, rows_per_cell)
    n1s_ref[...] = n1t_ref[pl.ds(first, rows_per_cell)]
    c1s_ref[...] = c1t_ref[pl.ds(first, rows_per_cell)]
    m_chunk = 128
    for m0 in range(0, eb, m_chunk):
        hid = jnp.dot(u_ref[m0:m0 + m_chunk, :], xnt_ref[...], preferred_element_type=F32).astype(BF16)
        for tc in range(n_tc):
            hid_ref[tc, m0:m0 + m_chunk, :] = hid[:, tc * LANES:(tc + 1) * LANES]
    _peer_elementwise(hid_ref, at_ref, n1s_ref, c1s_ref, r2_ref, e2_ref, tb=tb, eb=eb)
    at_full = jnp.concatenate([at_ref[tc] for tc in range(n_tc)], axis=1)
    acc_ref[...] += jnp.dot(vt_ref[...], at_full, preferred_element_type=F32)

    @pl.when(s == pl.num_programs(1) - 1)
    def _():
        h2 = h1_ref[...] + acc_ref[...].T
        y_ref[...] = _rms(h2, gfin_ref[...])


def _peer(h1, g_ffn, wq1t_bf, wq2t_bf, kperm_bf, keys2_bf, u_bf, vt_bf, g_final):
    t, d = h1.shape
    n_exp = u_bf.shape[0]
    tb, eb = 512, 1024
    full = lambda a: pl.BlockSpec(a.shape, lambda i, s: (0,) * a.ndim)
    route_t = pltpu.VMEM((PEER_NKEYS, PEER_HEADS, tb), F32)
    ranks = pltpu.VMEM((PEER_TOPK, PEER_HEADS, tb), F32)

    def chunked(*lead):
        return pltpu.VMEM(lead[:-1] + (tb // LANES, lead[-1], LANES), BF16)

    return pl.pallas_call(
        functools.partial(_peer_kernel, tb=tb, eb=eb),
        grid=(t // tb, n_exp // eb),
        in_specs=[pl.BlockSpec((tb, d), lambda i, s: (i, 0)),
                  full(g_ffn), full(wq1t_bf), full(wq2t_bf), full(kperm_bf), full(keys2_bf),
                  pl.BlockSpec((eb, d), lambda i, s: (s, 0)),
                  pl.BlockSpec((d, eb), lambda i, s: (0, s)),
                  full(g_final)],
        out_specs=pl.BlockSpec((tb, d), lambda i, s: (i, 0)),
        out_shape=jax.ShapeDtypeStruct((t, d), F32),
        scratch_shapes=[pltpu.VMEM((d, tb), BF16),
                        route_t,
                        route_t,
                        route_t,
                        pltpu.VMEM((PEER_HEADS, PEER_NKEYS, tb), F32),
                        chunked(PEER_HEADS, PEER_NKEYS),
                        chunked(PEER_HEADS, PEER_NKEYS),
                        ranks, ranks,
                        chunked(eb),
                        chunked(eb),
                        pltpu.VMEM((d, tb), F32),
                        pltpu.VMEM((eb // PEER_NKEYS, PEER_HEADS, tb), F32),
                        pltpu.VMEM((eb // PEER_NKEYS, PEER_HEADS, tb), F32)],
        compiler_params=pltpu.CompilerParams(dimension_semantics=("arbitrary", "arbitrary"),
                                             vmem_limit_bytes=PEER_VMEM_LIMIT),
    )(h1, g_ffn, wq1t_bf, wq2t_bf, kperm_bf, keys2_bf, u_bf, vt_bf, g_final)


def _block_diag(blocks):
    g, r, c = blocks.shape
    eye = jnp.eye(g, dtype=blocks.dtype)
    return jnp.einsum("grc,gk->grkc", blocks, eye).reshape(g * r, g * c)


def _peer_params(w_q, keys):
    d = w_q.shape[0]
    wq = w_q.reshape(d, PEER_HEADS, 2, PEER_HALF)
    wq1t = wq[:, :, 0, :].reshape(d, PEER_HEADS * PEER_HALF).T
    wq2t = wq[:, :, 1, :].reshape(d, PEER_HEADS * PEER_HALF).T
    eye = jnp.eye(PEER_HEADS, dtype=keys.dtype)
    kperm = jnp.einsum("hid,hk->ihkd", keys[:, 0], eye).reshape(PEER_NKEYS * PEER_HEADS, PEER_HEADS * PEER_HALF)
    return wq1t.astype(BF16), wq2t.astype(BF16), kperm.astype(BF16), keys[:, 1].astype(BF16)


def kernel(x, mem, positions, norm_mix, w_in, norm_mem, w_mem_kv, s5_a_re, s5_a_im, s5_log_dt, s5_b_re, s5_b_im, s5_c_re, s5_c_im, s5_d, s5_w_glu, w_branch_ret, w_branch_s5, w_branch_mem, w_out, norm_ffn, peer_w_q, peer_keys, peer_u, peer_v, norm_final):
    bsz, seq, d = x.shape
    depth = w_in.shape[0]
    assert depth == 1, "the final RMSNorm is fused into the PEER call of a single layer"
    assert bsz % SUBLANES == 0 and seq % 512 == 0 and d == 1024
    half = RET_DK // 2
    inv_freq = (ROPE_BASE ** (-jnp.arange(half, dtype=F32) / half)).reshape(1, half)
    posf = positions.astype(F32).reshape(bsz, seq, 1)
    row = lambda v: v.reshape(1, -1).astype(F32)

    h = x
    for l in range(depth):
        lam_re, lam_im, bb_re, bb_im = _s5_discretize(s5_a_re[l], s5_a_im[l], s5_log_dt[l], s5_b_re[l], s5_b_im[l])
        bmat = jnp.concatenate([_block_diag(bb_re), _block_diag(bb_im)], axis=1).astype(BF16)
        lam = jnp.stack([lam_re.reshape(-1), lam_im.reshape(-1)])
        c_re_t = jnp.transpose(s5_c_re[l], (0, 2, 1))
        c_im_t = jnp.transpose(s5_c_im[l], (0, 2, 1))
        cmat = jnp.concatenate([_block_diag(c_re_t), -_block_diag(c_im_t)], axis=0).astype(BF16)

        z = _inproj(h.reshape(bsz * seq, d), row(norm_mix[l]), w_in[l].astype(BF16))
        z3 = z.reshape(bsz, seq, IN_COLS)
        ret = _retention(z3, posf, inv_freq)
        u_sb = jnp.transpose(z3[:, :, COL_U:COL_U + S5_WIDTH], (1, 0, 2))
        ssm_sb = _s5(u_sb, bmat, lam, cmat, row(s5_d[l]), s5_w_glu[l].astype(BF16))
        ssm = jnp.transpose(ssm_sb, (1, 0, 2))
        kv = _memkv(mem, row(norm_mem[l]), w_mem_kv[l].astype(BF16))
        h1 = _merge(h, ret, ssm, z3, kv, w_branch_ret[l].astype(BF16), w_branch_s5[l].astype(BF16),
                    w_branch_mem[l].astype(BF16), w_out[l].astype(BF16))
        wq1t, wq2t, kperm, keys2 = _peer_params(peer_w_q[l], peer_keys[l])
        y = _peer(h1.reshape(bsz * seq, d), row(norm_ffn[l]), wq1t, wq2t, kperm, keys2,
                  peer_u[l].astype(BF16), peer_v[l].T.astype(BF16), row(norm_final))
        h = y.reshape(bsz, seq, d)
    return h
```

```python
import functools
import math

import jax
import jax.numpy as jnp
from jax import lax
from jax.experimental import pallas as pl
from jax.experimental.pallas import tpu as pltpu

F32 = jnp.float32
BF16 = jnp.bfloat16

EPS = 1e-6
RET_HEADS = 4
RET_DK = 128
RET_DV = 128
RET_QK_WIDTH = RET_HEADS * RET_DK
RET_V_WIDTH = RET_HEADS * RET_DV
ROPE_BASE = 10000.0
S5_GROUPS = 16
S5_GROUP_DIM = 16
S5_STATE = 64
S5_WIDTH = S5_GROUPS * S5_GROUP_DIM
S5_NSTATE = S5_GROUPS * S5_STATE
MEM_HEADS = 4
MEM_HEAD_DIM = 64
MEM_WIDTH = MEM_HEADS * MEM_HEAD_DIM
PEER_HEADS = 8
PEER_NKEYS = 128
PEER_HALF = 128
PEER_TOPK = 16

LANES = 128
SUBLANES = 8
VMEM_LIMIT = 48 * 1024 * 1024
PEER_VMEM_LIMIT = 56 * 1024 * 1024

COL_Q, COL_K, COL_V, COL_G = 0, 512, 1024, 1536
COL_U, COL_QM, COL_GATES = 2048, 2304, 2560
IN_COLS = 5632


def _cparams(n_axes):
    return pltpu.CompilerParams(dimension_semantics=("arbitrary",) * n_axes, vmem_limit_bytes=VMEM_LIMIT)


def _rms(x, gain):
    return x * lax.rsqrt(jnp.mean(x * x, axis=-1, keepdims=True) + EPS) * gain


def _gelu(x):
    c = math.sqrt(2.0 / math.pi)
    return 0.5 * x * (1.0 + jnp.tanh(c * (x + 0.044715 * (x * x * x))))


def _sigmoid(x):
    return 1.0 / (1.0 + jnp.exp(-x))


def _s5_disc_kernel(are_ref, aim_ref, ldt_ref, bre_ref, bim_ref, lre_ref, lim_ref, bbre_ref, bbim_ref):
    a_re = are_ref[...]
    a_im = aim_ref[...]
    dt = jnp.exp(ldt_ref[...])
    mag = jnp.exp(a_re * dt)
    lam_re = mag * jnp.cos(a_im * dt)
    lam_im = mag * jnp.sin(a_im * dt)
    den = a_re * a_re + a_im * a_im
    nr = lam_re - 1.0
    ni = lam_im
    cr = (nr * a_re + ni * a_im) / den
    ci = (ni * a_re - nr * a_im) / den
    b_re = bre_ref[...]
    b_im = bim_ref[...]
    lre_ref[...] = lam_re
    lim_ref[...] = lam_im
    bbre_ref[...] = cr * b_re - ci * b_im
    bbim_ref[...] = cr * b_im + ci * b_re


def _s5_discretize(a_re, a_im, log_dt, b_re, b_im):
    g, p = a_re.shape
    h = b_re.shape[-1]
    rep = lambda a: jnp.repeat(a, h, axis=0)
    b_re_t = jnp.transpose(b_re, (0, 2, 1)).reshape(g * h, p)
    b_im_t = jnp.transpose(b_im, (0, 2, 1)).reshape(g * h, p)
    shp = jax.ShapeDtypeStruct((g * h, p), F32)
    lam_re, lam_im, bb_re, bb_im = pl.pallas_call(
        _s5_disc_kernel, out_shape=(shp, shp, shp, shp),
    )(rep(a_re), rep(a_im), rep(log_dt[:, None]), b_re_t, b_im_t)
    return lam_re[::h], lam_im[::h], bb_re.reshape(g, h, p), bb_im.reshape(g, h, p)


def _inproj_kernel(x_ref, g_ref, w_ref, z_ref, xn_ref):
    @pl.when(pl.program_id(1) == 0)
    def _():
        xn_ref[...] = _rms(x_ref[...], g_ref[...]).astype(BF16)

    z_ref[...] = jnp.dot(xn_ref[...], w_ref[...], preferred_element_type=F32).astype(BF16)


def _inproj(x2, gain, w_bf):
    t, d = x2.shape
    tm, tn = 1024, 512
    return pl.pallas_call(
        _inproj_kernel,
        grid=(t // tm, IN_COLS // tn),
        in_specs=[pl.BlockSpec((tm, d), lambda i, j: (i, 0)),
                  pl.BlockSpec((1, d), lambda i, j: (0, 0)),
                  pl.BlockSpec((d, tn), lambda i, j: (0, j))],
        out_specs=pl.BlockSpec((tm, tn), lambda i, j: (i, j)),
        out_shape=jax.ShapeDtypeStruct((t, IN_COLS), BF16),
        scratch_shapes=[pltpu.VMEM((tm, d), BF16)],
        compiler_params=_cparams(2),
    )(x2, gain, w_bf)


def _ret_log_gamma(h):
    return math.log1p(-(2.0 ** (-5.0 - h)))


def _retention_kernel(q_ref, k_ref, v_ref, g_ref, pos_ref, invf_ref, o_ref,
                      state_ref, dmat_ref, qdec_ref, kdec_ref, *, lc):
    b = pl.program_id(0)
    c = pl.program_id(1)

    @pl.when((b == 0) & (c == 0))
    def _():
        row = lax.broadcasted_iota(jnp.int32, (lc, lc), 0)
        col = lax.broadcasted_iota(jnp.int32, (lc, lc), 1)
        rel = (row - col).astype(F32)
        pos = lax.broadcasted_iota(jnp.int32, (lc, RET_DK), 0).astype(F32)
        for h in range(RET_HEADS):
            lg = _ret_log_gamma(h)
            dmat_ref[h] = jnp.where(rel >= 0.0, jnp.exp(rel * lg), 0.0)
            qdec_ref[h] = jnp.exp((pos + 1.0) * lg)
            kdec_ref[h] = jnp.exp((lc - 1.0 - pos) * lg)

    @pl.when(c == 0)
    def _():
        state_ref[...] = jnp.zeros_like(state_ref)

    ang = pos_ref[...] * invf_ref[...]
    cs = jnp.cos(ang)
    sn = jnp.sin(ang)
    cc = jnp.concatenate([cs, cs], axis=-1)
    ss = jnp.concatenate([-sn, sn], axis=-1)
    half = RET_DK // 2
    for h in range(RET_HEADS):
        sl = slice(h * RET_DK, (h + 1) * RET_DK)
        qh = q_ref[:, sl].astype(F32)
        kh = k_ref[:, sl].astype(F32)
        qr = qh * cc + pltpu.roll(qh, half, 1) * ss
        kr = (kh * cc + pltpu.roll(kh, half, 1) * ss) * (RET_DK ** -0.5)
        vb = v_ref[:, sl]
        scores = lax.dot_general(qr.astype(BF16), kr.astype(BF16), (((1,), (1,)), ((), ())),
                                 preferred_element_type=F32) * dmat_ref[h]
        inner = jnp.dot(scores.astype(BF16), vb, preferred_element_type=F32)
        st = state_ref[h]
        cross = jnp.dot((qr * qdec_ref[h]).astype(BF16), st.astype(BF16), preferred_element_type=F32)
        kdt = (kr * kdec_ref[h]).T.astype(BF16)
        state_ref[h] = st * math.exp(lc * _ret_log_gamma(h)) + jnp.dot(kdt, vb, preferred_element_type=F32)
        o = inner + cross
        o = o * lax.rsqrt(jnp.mean(o * o, axis=-1, keepdims=True) + EPS)
        gh = g_ref[:, sl].astype(F32)
        o_ref[:, sl] = o * (gh * _sigmoid(gh))


def _retention(z3, posf, inv_freq):
    bsz, seq, _ = z3.shape
    lc = 256
    blk = lambda k: pl.BlockSpec((None, lc, RET_QK_WIDTH), lambda b, c, k=k: (b, c, k))
    return pl.pallas_call(
        functools.partial(_retention_kernel, lc=lc),
        grid=(bsz, seq // lc),
        in_specs=[blk(COL_Q // 512), blk(COL_K // 512), blk(COL_V // 512), blk(COL_G // 512),
                  pl.BlockSpec((None, lc, 1), lambda b, c: (b, c, 0)),
                  pl.BlockSpec((1, RET_DK // 2), lambda b, c: (0, 0))],
        out_specs=pl.BlockSpec((None, lc, RET_V_WIDTH), lambda b, c: (b, c, 0)),
        out_shape=jax.ShapeDtypeStruct((bsz, seq, RET_V_WIDTH), F32),
        scratch_shapes=[pltpu.VMEM((RET_HEADS, RET_DK, RET_DV), F32),
                        pltpu.VMEM((RET_HEADS, lc, lc), F32),
                        pltpu.VMEM((RET_HEADS, lc, RET_DK), F32),
                        pltpu.VMEM((RET_HEADS, lc, RET_DK), F32)],
        compiler_params=_cparams(2),
    )(z3, z3, z3, z3, posf, inv_freq)


S5_SCAN_COLS = 512


def _s5_kernel(u_ref, bmat_ref, lam_ref, cmat_ref, d_ref, wglu_ref, o_ref, x_ref, state_ref, *, ts, bsz):
    @pl.when(pl.program_id(0) == 0)
    def _():
        state_ref[...] = jnp.zeros_like(state_ref)

    u = u_ref[...].reshape(ts * bsz, S5_WIDTH)
    x_ref[...] = jnp.dot(u.astype(BF16), bmat_ref[...], preferred_element_type=F32)

    for c0 in range(0, S5_NSTATE, S5_SCAN_COLS):
        re = pl.ds(c0, S5_SCAN_COLS)
        im = pl.ds(S5_NSTATE + c0, S5_SCAN_COLS)
        lr = jnp.broadcast_to(lam_ref[0:1, c0:c0 + S5_SCAN_COLS], (bsz, S5_SCAN_COLS))
        li = jnp.broadcast_to(lam_ref[1:2, c0:c0 + S5_SCAN_COLS], (bsz, S5_SCAN_COLS))

        def step(t, carry):
            xr, xi = carry
            rows = pl.ds(pl.multiple_of(t * bsz, bsz), bsz)
            nr = lr * xr - li * xi + x_ref[rows, re]
            ni = lr * xi + li * xr + x_ref[rows, im]
            x_ref[rows, re] = nr
            x_ref[rows, im] = ni
            return nr, ni

        xr, xi = lax.fori_loop(0, ts, step, (state_ref[:, re], state_ref[:, im]), unroll=8)
        state_ref[:, re] = xr
        state_ref[:, im] = xi

    y = jnp.dot(x_ref[...].astype(BF16), cmat_ref[...], preferred_element_type=F32) + d_ref[...] * u
    y = _gelu(y)
    lg = jnp.dot(y.astype(BF16), wglu_ref[...], preferred_element_type=F32)
    out = lg[:, :S5_WIDTH] * _sigmoid(lg[:, S5_WIDTH:])
    o_ref[...] = out.reshape(ts, bsz, S5_WIDTH)


def _s5(u_sb, bmat, lam, cmat, d, wglu_bf):
    seq, bsz, _ = u_sb.shape
    ts = 128
    full = lambda a: pl.BlockSpec(a.shape, lambda i: (0,) * a.ndim)
    return pl.pallas_call(
        functools.partial(_s5_kernel, ts=ts, bsz=bsz),
        grid=(seq // ts,),
        in_specs=[pl.BlockSpec((ts, bsz, S5_WIDTH), lambda i: (i, 0, 0)),
                  full(bmat), full(lam), full(cmat), full(d), full(wglu_bf)],
        out_specs=pl.BlockSpec((ts, bsz, S5_WIDTH), lambda i: (i, 0, 0)),
        out_shape=jax.ShapeDtypeStruct((seq, bsz, S5_WIDTH), F32),
        scratch_shapes=[pltpu.VMEM((ts * bsz, 2 * S5_NSTATE), F32),
                        pltpu.VMEM((bsz, 2 * S5_NSTATE), F32)],
        compiler_params=_cparams(1),
    )(u_sb, bmat, lam, cmat, d, wglu_bf)


def _memkv_kernel(m_ref, g_ref, w_ref, o_ref):
    mn = _rms(m_ref[...], g_ref[...]).astype(BF16)
    o_ref[...] = jnp.dot(mn, w_ref[...], preferred_element_type=F32)


def _memkv(mem, gain, w_bf):
    bsz, m, d = mem.shape
    n = w_bf.shape[1]
    return pl.pallas_call(
        _memkv_kernel,
        grid=(bsz,),
        in_specs=[pl.BlockSpec((None, m, d), lambda b: (b, 0, 0)),
                  pl.BlockSpec((1, d), lambda b: (0, 0)),
                  pl.BlockSpec((d, n), lambda b: (0, 0))],
        out_specs=pl.BlockSpec((None, m, n), lambda b: (b, 0, 0)),
        out_shape=jax.ShapeDtypeStruct((bsz, m, n), F32),
        compiler_params=_cparams(1),
    )(mem, gain, w_bf)


def _merge_kernel(x_ref, ret_ref, ssm_ref, qm_ref, g0, g1, g2, g3, g4, g5, kv_ref,
                  wr_ref, ws_ref, wm_ref, wo_ref, o_ref):
    kv = kv_ref[...]
    q = qm_ref[...]
    heads = []
    for h in range(MEM_HEADS):
        sl = slice(h * MEM_HEAD_DIM, (h + 1) * MEM_HEAD_DIM)
        kh = kv[:, sl].astype(BF16)
        vh = kv[:, MEM_WIDTH + h * MEM_HEAD_DIM:MEM_WIDTH + (h + 1) * MEM_HEAD_DIM].astype(BF16)
        s = lax.dot_general(q[:, sl].astype(BF16), kh, (((1,), (1,)), ((), ())),
                            preferred_element_type=F32) * (MEM_HEAD_DIM ** -0.5)
        s = s - jnp.max(s, axis=-1, keepdims=True)
        p = jnp.exp(s)
        p = p / jnp.sum(p, axis=-1, keepdims=True)
        heads.append(jnp.dot(p.astype(BF16), vh, preferred_element_type=F32))
    mattn = jnp.concatenate(heads, axis=-1)

    pr = jnp.dot(ret_ref[...].astype(BF16), wr_ref[...], preferred_element_type=F32)
    ps = jnp.dot(ssm_ref[...].astype(BF16), ws_ref[...], preferred_element_type=F32)
    pm = jnp.dot(mattn.astype(BF16), wm_ref[...], preferred_element_type=F32)
    gr = jnp.concatenate([g0[...], g1[...]], axis=-1).astype(F32)
    gs = jnp.concatenate([g2[...], g3[...]], axis=-1).astype(F32)
    gm = jnp.concatenate([g4[...], g5[...]], axis=-1).astype(F32)
    merged = _sigmoid(gr) * pr + _sigmoid(gs) * ps + _sigmoid(gm) * pm
    o_ref[...] = x_ref[...] + jnp.dot(merged.astype(BF16), wo_ref[...], preferred_element_type=F32)


def _merge(x, ret, ssm, z3, kv, wr, ws, wm, wo):
    bsz, seq, d = x.shape
    tm = 512
    gw = 512
    gate = lambda k: pl.BlockSpec((None, tm, gw), lambda b, i, k=k: (b, i, COL_GATES // gw + k))
    full = lambda a: pl.BlockSpec(a.shape, lambda b, i: (0,) * a.ndim)
    return pl.pallas_call(
        _merge_kernel,
        grid=(bsz, seq // tm),
        in_specs=[pl.BlockSpec((None, tm, d), lambda b, i: (b, i, 0)),
                  pl.BlockSpec((None, tm, RET_V_WIDTH), lambda b, i: (b, i, 0)),
                  pl.BlockSpec((None, tm, S5_WIDTH), lambda b, i: (b, i, 0)),
                  pl.BlockSpec((None, tm, MEM_WIDTH), lambda b, i: (b, i, COL_QM // MEM_WIDTH)),
                  gate(0), gate(1), gate(2), gate(3), gate(4), gate(5),
                  pl.BlockSpec((None,) + kv.shape[1:], lambda b, i: (b, 0, 0)),
                  full(wr), full(ws), full(wm), full(wo)],
        out_specs=pl.BlockSpec((None, tm, d), lambda b, i: (b, i, 0)),
        out_shape=jax.ShapeDtypeStruct((bsz, seq, d), F32),
        compiler_params=_cparams(2),
    )(x, ret, ssm, z3, z3, z3, z3, z3, z3, z3, kv, wr, ws, wm, wo)


NEG_INF = float("-inf")
PEER_CAND = [(i, j) for i in range(PEER_TOPK) for j in range(PEER_TOPK) if (i + 1) * (j + 1) <= PEER_TOPK]


def _bitonic_merge_desc(v):
    v = list(v)
    j = len(v) // 2
    while j >= 1:
        for i in range(len(v)):
            l = i ^ j
            if l > i:
                v[i], v[l] = jnp.maximum(v[i], v[l]), jnp.minimum(v[i], v[l])
        j //= 2
    return v


def _sort_desc(v):
    v = list(v)
    n = len(v)
    k = 2
    while k <= n:
        j = k // 2
        while j >= 1:
            for i in range(n):
                l = i ^ j
                if l > i:
                    hi, lo = jnp.maximum(v[i], v[l]), jnp.minimum(v[i], v[l])
                    v[i], v[l] = (hi, lo) if (i & k) == 0 else (lo, hi)
            j //= 2
        k *= 2
    return v


def _peer_route_chunk(tc, s1t_ref, s2_ref, c1t_ref, n1t_ref, r2_ref, e2_ref, a_ref, b_ref):
    k = PEER_TOPK
    lanes = pl.ds(pl.multiple_of(tc * LANES, LANES), LANES)
    top = _sort_desc([s1t_ref[i, :, lanes] for i in range(k)])
    for g in range(1, PEER_NKEYS // k):
        grp = _sort_desc([s1t_ref[g * k + i, :, lanes] for i in range(k)])
        top = _bitonic_merge_desc([jnp.maximum(top[i], grp[k - 1 - i]) for i in range(k)])
    for r in range(k):
        a_ref[r, :, lanes] = top[r]
    for h in range(PEER_HEADS):
        s = s2_ref[h, :, lanes]
        rank = jnp.full(s.shape, float(k), F32)
        for r in range(k):
            m = jnp.max(s, axis=0, keepdims=True)
            b_ref[r, pl.ds(h, 1), lanes] = m
            hit = s == m
            rank = jnp.where(hit, float(r), rank)
            s = jnp.where(hit, NEG_INF, s)
        r2_ref[h, tc] = rank.astype(BF16)
    a = [a_ref[r, :, lanes] for r in range(k)]
    b = [b_ref[r, :, lanes] for r in range(k)]
    ea = [jnp.exp(v - a[0]) for v in a]
    eb = [jnp.exp(v - b[0]) for v in b]
    cand = [a[i] + b[j] for i, j in PEER_CAND]
    work = cand
    for r in range(k):
        tau = functools.reduce(jnp.maximum, work)
        if r + 1 < k:
            work = [jnp.where(w == tau, NEG_INF, w) for w in work]
    zsum = functools.reduce(
        jnp.add, [jnp.where(c >= tau, ea[i] * eb[j], 0.0) for c, (i, j) in zip(cand, PEER_CAND)])
    cnt = [functools.reduce(jnp.add, [jnp.where(a[i] + b[j] >= tau, 1.0, 0.0) for j in range(k)]) for i in range(k)]
    inv_z = 1.0 / zsum

    def group(g, carry):
        rows = pl.ds(pl.multiple_of(g * SUBLANES, SUBLANES), SUBLANES)
        s1 = s1t_ref[rows, :, lanes]
        n1 = jnp.zeros(s1.shape, F32)
        for i in range(k):
            n1 = jnp.where(s1 == a[i][None], cnt[i][None], n1)
        n1t_ref[rows, :, lanes] = n1
        c1t_ref[rows, :, lanes] = jnp.exp(s1 - a[0][None]) * inv_z[None]
        return carry

    lax.fori_loop(0, PEER_NKEYS // SUBLANES, group, 0)
    for h in range(PEER_HEADS):
        e2_ref[h, tc] = jnp.exp(s2_ref[h, :, lanes] - b[0][h:h + 1, :]).astype(BF16)


BF16_ROWS = 16


def _gelu_bf16(x):
    c = math.sqrt(2.0 / math.pi)
    u = x * (c + (c * 0.044715) * (x * x))
    return (0.5 * x) * (1.0 + jnp.tanh(u))


def _peer_elementwise(hid_ref, at_ref, n1s_ref, c1s_ref, r2_ref, e2_ref, *, tb, eb):
    n_slab = PEER_NKEYS // BF16_ROWS
    for r in range(eb // PEER_NKEYS):
        for tc in range(tb // LANES):
            lanes = pl.ds(tc * LANES, LANES)
            w = [None] * n_slab
            for h in range(PEER_HEADS):
                nb = jnp.broadcast_to(n1s_ref[r, pl.ds(h, 1), lanes], (BF16_ROWS, LANES)).astype(BF16)
                cb = jnp.broadcast_to(c1s_ref[r, pl.ds(h, 1), lanes], (BF16_ROWS, LANES)).astype(BF16)
                for k in range(n_slab):
                    rows = pl.ds(k * BF16_ROWS, BF16_ROWS)
                    contrib = jnp.where(r2_ref[h, tc, rows, :] < nb, e2_ref[h, tc, rows, :] * cb, 0)
                    w[k] = contrib if w[k] is None else w[k] + contrib
            for k in range(n_slab):
                rows = pl.ds(r * PEER_NKEYS + k * BF16_ROWS, BF16_ROWS)
                at_ref[tc, rows, :] = _gelu_bf16(hid_ref[tc, rows, :]) * w[k]


def _peer_kernel(h1_ref, gf_ref, wq1t_ref, wq2t_ref, kperm_ref, keys2_ref, u_ref, vt_ref, gfin_ref, y_ref,
                 xnt_ref, s1t_ref, c1t_ref, n1t_ref, s2_ref, r2_ref, e2_ref, a_ref, b_ref,
                 hid_ref, at_ref, acc_ref, n1s_ref, c1s_ref, *, tb, eb):
    s = pl.program_id(1)
    rows_per_cell = eb // PEER_NKEYS
    n_tc = tb // LANES

    @pl.when(s == 0)
    def _():
        xn = _rms(h1_ref[...], gf_ref[...])
        xnt_ref[...] = xn.T.astype(BF16)
        q1t = jnp.dot(wq1t_ref[...], xnt_ref[...], preferred_element_type=F32).astype(BF16)
        q2t = jnp.dot(wq2t_ref[...], xnt_ref[...], preferred_element_type=F32).astype(BF16)
        s1t = jnp.dot(kperm_ref[...], q1t, preferred_element_type=F32)
        s1t_ref[...] = s1t.reshape(PEER_NKEYS, PEER_HEADS, tb)
        for h in range(PEER_HEADS):
            s2_ref[h] = jnp.dot(keys2_ref[h], q2t[h * PEER_HALF:(h + 1) * PEER_HALF],
                                preferred_element_type=F32)
        def route(tc, carry):
            _peer_route_chunk(tc, s1t_ref, s2_ref, c1t_ref, n1t_ref, r2_ref, e2_ref, a_ref, b_ref)
            return carry

        lax.fori_loop(0, tb // LANES, route, 0)
        acc_ref[...] = jnp.zeros_like(acc_ref)

    first = pl.multiple_of(s * rows_per_cell, rows_per_cell)
    n1s_ref[...] = n1t_ref[pl.ds(first, rows_per_cell)]
    c1s_ref[...] = c1t_ref[pl.ds(first, rows_per_cell)]
    m_chunk = 256
    for m0 in range(0, eb, m_chunk):
        hid = jnp.dot(u_ref[m0:m0 + m_chunk, :], xnt_ref[...], preferred_element_type=F32).astype(BF16)
        for tc in range(n_tc):
            hid_ref[tc, m0:m0 + m_chunk, :] = hid[:, tc * LANES:(tc + 1) * LANES]
    _peer_elementwise(hid_ref, at_ref, n1s_ref, c1s_ref, r2_ref, e2_ref, tb=tb, eb=eb)
    at_full = jnp.concatenate([at_ref[tc] for tc in range(n_tc)], axis=1)
    acc_ref[...] += jnp.dot(vt_ref[...], at_full, preferred_element_type=F32)

    @pl.when(s == pl.num_programs(1) - 1)
    def _():
        h2 = h1_ref[...] + acc_ref[...].T
        y_ref[...] = _rms(h2, gfin_ref[...])


def _peer(h1, g_ffn, wq1t_bf, wq2t_bf, kperm_bf, keys2_bf, u_bf, vt_bf, g_final):
    t, d = h1.shape
    n_exp = u_bf.shape[0]
    tb, eb = 512, 1024
    full = lambda a: pl.BlockSpec(a.shape, lambda i, s: (0,) * a.ndim)
    route_t = pltpu.VMEM((PEER_NKEYS, PEER_HEADS, tb), F32)
    ranks = pltpu.VMEM((PEER_TOPK, PEER_HEADS, tb), F32)

    def chunked(*lead):
        return pltpu.VMEM(lead[:-1] + (tb // LANES, lead[-1], LANES), BF16)

    return pl.pallas_call(
        functools.partial(_peer_kernel, tb=tb, eb=eb),
        grid=(t // tb, n_exp // eb),
        in_specs=[pl.BlockSpec((tb, d), lambda i, s: (i, 0)),
                  full(g_ffn), full(wq1t_bf), full(wq2t_bf), full(kperm_bf), full(keys2_bf),
                  pl.BlockSpec((eb, d), lambda i, s: (s, 0)),
                  pl.BlockSpec((d, eb), lambda i, s: (0, s)),
                  full(g_final)],
        out_specs=pl.BlockSpec((tb, d), lambda i, s: (i, 0)),
        out_shape=jax.ShapeDtypeStruct((t, d), F32),
        scratch_shapes=[pltpu.VMEM((d, tb), BF16),
                        route_t,
                        route_t,
                        route_t,
                        pltpu.VMEM((PEER_HEADS, PEER_NKEYS, tb), F32),
                        chunked(PEER_HEADS, PEER_NKEYS),
                        chunked(PEER_HEADS, PEER_NKEYS),
                        ranks, ranks,
                        chunked(eb),
                        chunked(eb),
                        pltpu.VMEM((d, tb), F32),
                        pltpu.VMEM((eb // PEER_NKEYS, PEER_HEADS, tb), F32),
                        pltpu.VMEM((eb // PEER_NKEYS, PEER_HEADS, tb), F32)],
        compiler_params=pltpu.CompilerParams(dimension_semantics=("arbitrary", "arbitrary"),
                                             vmem_limit_bytes=PEER_VMEM_LIMIT),
    )(h1, g_ffn, wq1t_bf, wq2t_bf, kperm_bf, keys2_bf, u_bf, vt_bf, g_final)


def _block_diag(blocks):
    g, r, c = blocks.shape
    eye = jnp.eye(g, dtype=blocks.dtype)
    return jnp.einsum("grc,gk->grkc", blocks, eye).reshape(g * r, g * c)


def _peer_params(w_q, keys):
    d = w_q.shape[0]
    wq = w_q.reshape(d, PEER_HEADS, 2, PEER_HALF)
    wq1t = wq[:, :, 0, :].reshape(d, PEER_HEADS * PEER_HALF).T
    wq2t = wq[:, :, 1, :].reshape(d, PEER_HEADS * PEER_HALF).T
    eye = jnp.eye(PEER_HEADS, dtype=keys.dtype)
    kperm = jnp.einsum("hid,hk->ihkd", keys[:, 0], eye).reshape(PEER_NKEYS * PEER_HEADS, PEER_HEADS * PEER_HALF)
    return wq1t.astype(BF16), wq2t.astype(BF16), kperm.astype(BF16), keys[:, 1].astype(BF16)


def kernel(x, mem, positions, norm_mix, w_in, norm_mem, w_mem_kv, s5_a_re, s5_a_im, s5_log_dt, s5_b_re, s5_b_im, s5_c_re, s5_c_im, s5_d, s5_w_glu, w_branch_ret, w_branch_s5, w_branch_mem, w_out, norm_ffn, peer_w_q, peer_keys, peer_u, peer_v, norm_final):
    bsz, seq, d = x.shape
    depth = w_in.shape[0]
    assert depth == 1, "the final RMSNorm is fused into the PEER call of a single layer"
    assert bsz % SUBLANES == 0 and seq % 512 == 0 and d == 1024
    half = RET_DK // 2
    inv_freq = (ROPE_BASE ** (-jnp.arange(half, dtype=F32) / half)).reshape(1, half)
    posf = positions.astype(F32).reshape(bsz, seq, 1)
    row = lambda v: v.reshape(1, -1).astype(F32)

    h = x
    for l in range(depth):
        lam_re, lam_im, bb_re, bb_im = _s5_discretize(s5_a_re[l], s5_a_im[l], s5_log_dt[l], s5_b_re[l], s5_b_im[l])
        bmat = jnp.concatenate([_block_diag(bb_re), _block_diag(bb_im)], axis=1).astype(BF16)
        lam = jnp.stack([lam_re.reshape(-1), lam_im.reshape(-1)])
        c_re_t = jnp.transpose(s5_c_re[l], (0, 2, 1))
        c_im_t = jnp.transpose(s5_c_im[l], (0, 2, 1))
        cmat = jnp.concatenate([_block_diag(c_re_t), -_block_diag(c_im_t)], axis=0).astype(BF16)

        z = _inproj(h.reshape(bsz * seq, d), row(norm_mix[l]), w_in[l].astype(BF16))
        z3 = z.reshape(bsz, seq, IN_COLS)
        ret = _retention(z3, posf, inv_freq)
        u_sb = jnp.transpose(z3[:, :, COL_U:COL_U + S5_WIDTH], (1, 0, 2)).astype(F32)
        ssm_sb = _s5(u_sb, bmat, lam, cmat, row(s5_d[l]), s5_w_glu[l].astype(BF16))
        ssm = jnp.transpose(ssm_sb, (1, 0, 2))
        kv = _memkv(mem, row(norm_mem[l]), w_mem_kv[l].astype(BF16))
        h1 = _merge(h, ret, ssm, z3, kv, w_branch_ret[l].astype(BF16), w_branch_s5[l].astype(BF16),
                    w_branch_mem[l].astype(BF16), w_out[l].astype(BF16))
        wq1t, wq2t, kperm, keys2 = _peer_params(peer_w_q[l], peer_keys[l])
        y = _peer(h1.reshape(bsz * seq, d), row(norm_ffn[l]), wq1t, wq2t, kperm, keys2,
                  peer_u[l].astype(BF16), peer_v[l].T.astype(BF16), row(norm_final))
        h = y.reshape(bsz, seq, d)
    return h
```

```python
import functools
import math

import jax
import jax.numpy as jnp
from jax import lax
from jax.experimental import pallas as pl
from jax.experimental.pallas import tpu as pltpu

F32 = jnp.float32
BF16 = jnp.bfloat16

EPS = 1e-6
RET_HEADS = 4
RET_DK = 128
RET_DV = 128
RET_QK_WIDTH = RET_HEADS * RET_DK
RET_V_WIDTH = RET_HEADS * RET_DV
ROPE_BASE = 10000.0
S5_GROUPS = 16
S5_GROUP_DIM = 16
S5_STATE = 64
S5_WIDTH = S5_GROUPS * S5_GROUP_DIM
S5_NSTATE = S5_GROUPS * S5_STATE
MEM_HEADS = 4
MEM_HEAD_DIM = 64
MEM_WIDTH = MEM_HEADS * MEM_HEAD_DIM
PEER_HEADS = 8
PEER_NKEYS = 128
PEER_HALF = 128
PEER_TOPK = 16

LANES = 128
SUBLANES = 8
VMEM_LIMIT = 48 * 1024 * 1024
PEER_VMEM_LIMIT = 60 * 1024 * 1024

COL_Q, COL_K, COL_V, COL_G = 0, 512, 1024, 1536
COL_U, COL_QM, COL_GATES = 2048, 2304, 2560
IN_COLS = 5632


def _cparams(n_axes):
    return pltpu.CompilerParams(dimension_semantics=("arbitrary",) * n_axes, vmem_limit_bytes=VMEM_LIMIT)


def _rms(x, gain):
    return x * lax.rsqrt(jnp.mean(x * x, axis=-1, keepdims=True) + EPS) * gain


def _gelu(x):
    c = math.sqrt(2.0 / math.pi)
    return 0.5 * x * (1.0 + jnp.tanh(c * (x + 0.044715 * (x * x * x))))


def _sigmoid(x):
    return 1.0 / (1.0 + jnp.exp(-x))


def _s5_disc_kernel(are_ref, aim_ref, ldt_ref, bre_ref, bim_ref, lre_ref, lim_ref, bbre_ref, bbim_ref):
    a_re = are_ref[...]
    a_im = aim_ref[...]
    dt = jnp.exp(ldt_ref[...])
    mag = jnp.exp(a_re * dt)
    lam_re = mag * jnp.cos(a_im * dt)
    lam_im = mag * jnp.sin(a_im * dt)
    den = a_re * a_re + a_im * a_im
    nr = lam_re - 1.0
    ni = lam_im
    cr = (nr * a_re + ni * a_im) / den
    ci = (ni * a_re - nr * a_im) / den
    b_re = bre_ref[...]
    b_im = bim_ref[...]
    lre_ref[...] = lam_re
    lim_ref[...] = lam_im
    bbre_ref[...] = cr * b_re - ci * b_im
    bbim_ref[...] = cr * b_im + ci * b_re


def _s5_discretize(a_re, a_im, log_dt, b_re, b_im):
    g, p = a_re.shape
    h = b_re.shape[-1]
    rep = lambda a: jnp.repeat(a, h, axis=0)
    b_re_t = jnp.transpose(b_re, (0, 2, 1)).reshape(g * h, p)
    b_im_t = jnp.transpose(b_im, (0, 2, 1)).reshape(g * h, p)
    shp = jax.ShapeDtypeStruct((g * h, p), F32)
    lam_re, lam_im, bb_re, bb_im = pl.pallas_call(
        _s5_disc_kernel, out_shape=(shp, shp, shp, shp),
    )(rep(a_re), rep(a_im), rep(log_dt[:, None]), b_re_t, b_im_t)
    return lam_re[::h], lam_im[::h], bb_re.reshape(g, h, p), bb_im.reshape(g, h, p)


def _inproj_kernel(x_ref, g_ref, w_ref, z_ref, xn_ref):
    @pl.when(pl.program_id(1) == 0)
    def _():
        xn_ref[...] = _rms(x_ref[...], g_ref[...]).astype(BF16)

    z_ref[...] = jnp.dot(xn_ref[...], w_ref[...], preferred_element_type=F32).astype(BF16)


def _inproj(x2, gain, w_bf):
    t, d = x2.shape
    tm, tn = 512, IN_COLS
    return pl.pallas_call(
        _inproj_kernel,
        grid=(t // tm, IN_COLS // tn),
        in_specs=[pl.BlockSpec((tm, d), lambda i, j: (i, 0)),
                  pl.BlockSpec((1, d), lambda i, j: (0, 0)),
                  pl.BlockSpec((d, tn), lambda i, j: (0, j), pipeline_mode=pl.Buffered(1))],
        out_specs=pl.BlockSpec((tm, tn), lambda i, j: (i, j)),
        out_shape=jax.ShapeDtypeStruct((t, IN_COLS), BF16),
        scratch_shapes=[pltpu.VMEM((tm, d), BF16)],
        compiler_params=_cparams(2),
    )(x2, gain, w_bf)


def _ret_log_gamma(h):
    return math.log1p(-(2.0 ** (-5.0 - h)))


def _retention_kernel(q_ref, k_ref, v_ref, g_ref, pos_ref, invf_ref, o_ref,
                      state_ref, dmat_ref, qdec_ref, kdec_ref, *, lc):
    b = pl.program_id(0)
    c = pl.program_id(1)

    @pl.when((b == 0) & (c == 0))
    def _():
        row = lax.broadcasted_iota(jnp.int32, (lc, lc), 0)
        col = lax.broadcasted_iota(jnp.int32, (lc, lc), 1)
        rel = (row - col).astype(F32)
        pos = lax.broadcasted_iota(jnp.int32, (lc, RET_DK), 0).astype(F32)
        for h in range(RET_HEADS):
            lg = _ret_log_gamma(h)
            dmat_ref[h] = jnp.where(rel >= 0.0, jnp.exp(rel * lg), 0.0)
            qdec_ref[h] = jnp.exp((pos + 1.0) * lg)
            kdec_ref[h] = jnp.exp((lc - 1.0 - pos) * lg)

    @pl.when(c == 0)
    def _():
        state_ref[...] = jnp.zeros_like(state_ref)

    ang = pos_ref[...] * invf_ref[...]
    cs = jnp.cos(ang)
    sn = jnp.sin(ang)
    cc = jnp.concatenate([cs, cs], axis=-1)
    ss = jnp.concatenate([-sn, sn], axis=-1)
    half = RET_DK // 2
    for h in range(RET_HEADS):
        sl = slice(h * RET_DK, (h + 1) * RET_DK)
        qh = q_ref[:, sl].astype(F32)
        kh = k_ref[:, sl].astype(F32)
        qr = qh * cc + pltpu.roll(qh, half, 1) * ss
        kr = (kh * cc + pltpu.roll(kh, half, 1) * ss) * (RET_DK ** -0.5)
        vb = v_ref[:, sl]
        scores = lax.dot_general(qr.astype(BF16), kr.astype(BF16), (((1,), (1,)), ((), ())),
                                 preferred_element_type=F32) * dmat_ref[h]
        inner = jnp.dot(scores.astype(BF16), vb, preferred_element_type=F32)
        st = state_ref[h]
        cross = jnp.dot((qr * qdec_ref[h]).astype(BF16), st.astype(BF16), preferred_element_type=F32)
        kdt = (kr * kdec_ref[h]).T.astype(BF16)
        state_ref[h] = st * math.exp(lc * _ret_log_gamma(h)) + jnp.dot(kdt, vb, preferred_element_type=F32)
        o = inner + cross
        o = o * lax.rsqrt(jnp.mean(o * o, axis=-1, keepdims=True) + EPS)
        gh = g_ref[:, sl].astype(F32)
        o_ref[:, sl] = o * (gh * _sigmoid(gh))


def _retention(z3, posf, inv_freq):
    bsz, seq, _ = z3.shape
    lc = 256
    blk = lambda k: pl.BlockSpec((None, lc, RET_QK_WIDTH), lambda b, c, k=k: (b, c, k))
    return pl.pallas_call(
        functools.partial(_retention_kernel, lc=lc),
        grid=(bsz, seq // lc),
        in_specs=[blk(COL_Q // 512), blk(COL_K // 512), blk(COL_V // 512), blk(COL_G // 512),
                  pl.BlockSpec((None, lc, 1), lambda b, c: (b, c, 0)),
                  pl.BlockSpec((1, RET_DK // 2), lambda b, c: (0, 0))],
        out_specs=pl.BlockSpec((None, lc, RET_V_WIDTH), lambda b, c: (b, c, 0)),
        out_shape=jax.ShapeDtypeStruct((bsz, seq, RET_V_WIDTH), F32),
        scratch_shapes=[pltpu.VMEM((RET_HEADS, RET_DK, RET_DV), F32),
                        pltpu.VMEM((RET_HEADS, lc, lc), F32),
                        pltpu.VMEM((RET_HEADS, lc, RET_DK), F32),
                        pltpu.VMEM((RET_HEADS, lc, RET_DK), F32)],
        compiler_params=_cparams(2),
    )(z3, z3, z3, z3, posf, inv_freq)


S5_SCAN_COLS = 512


def _s5_kernel(u_ref, bmat_ref, lam_ref, cmat_ref, d_ref, wglu_ref, o_ref, x_ref, state_ref, *, ts, bsz):
    @pl.when(pl.program_id(0) == 0)
    def _():
        state_ref[...] = jnp.zeros_like(state_ref)

    u = u_ref[...].reshape(ts * bsz, S5_WIDTH)
    x_ref[...] = jnp.dot(u.astype(BF16), bmat_ref[...], preferred_element_type=F32)

    for c0 in range(0, S5_NSTATE, S5_SCAN_COLS):
        re = pl.ds(c0, S5_SCAN_COLS)
        im = pl.ds(S5_NSTATE + c0, S5_SCAN_COLS)
        lr = jnp.broadcast_to(lam_ref[0:1, c0:c0 + S5_SCAN_COLS], (bsz, S5_SCAN_COLS))
        li = jnp.broadcast_to(lam_ref[1:2, c0:c0 + S5_SCAN_COLS], (bsz, S5_SCAN_COLS))

        def step(t, carry):
            xr, xi = carry
            rows = pl.ds(pl.multiple_of(t * bsz, bsz), bsz)
            nr = lr * xr - li * xi + x_ref[rows, re]
            ni = lr * xi + li * xr + x_ref[rows, im]
            x_ref[rows, re] = nr
            x_ref[rows, im] = ni
            return nr, ni

        xr, xi = lax.fori_loop(0, ts, step, (state_ref[:, re], state_ref[:, im]), unroll=8)
        state_ref[:, re] = xr
        state_ref[:, im] = xi

    y = jnp.dot(x_ref[...].astype(BF16), cmat_ref[...], preferred_element_type=F32) + d_ref[...] * u
    y = _gelu(y)
    lg = jnp.dot(y.astype(BF16), wglu_ref[...], preferred_element_type=F32)
    out = lg[:, :S5_WIDTH] * _sigmoid(lg[:, S5_WIDTH:])
    o_ref[...] = out.reshape(ts, bsz, S5_WIDTH)


def _s5(u_sb, bmat, lam, cmat, d, wglu_bf):
    seq, bsz, _ = u_sb.shape
    ts = 256
    full = lambda a: pl.BlockSpec(a.shape, lambda i: (0,) * a.ndim)
    return pl.pallas_call(
        functools.partial(_s5_kernel, ts=ts, bsz=bsz),
        grid=(seq // ts,),
        in_specs=[pl.BlockSpec((ts, bsz, S5_WIDTH), lambda i: (i, 0, 0)),
                  full(bmat), full(lam), full(cmat), full(d), full(wglu_bf)],
        out_specs=pl.BlockSpec((ts, bsz, S5_WIDTH), lambda i: (i, 0, 0)),
        out_shape=jax.ShapeDtypeStruct((seq, bsz, S5_WIDTH), F32),
        scratch_shapes=[pltpu.VMEM((ts * bsz, 2 * S5_NSTATE), F32),
                        pltpu.VMEM((bsz, 2 * S5_NSTATE), F32)],
        compiler_params=_cparams(1),
    )(u_sb, bmat, lam, cmat, d, wglu_bf)


def _memkv_kernel(m_ref, g_ref, w_ref, o_ref):
    mn = _rms(m_ref[...], g_ref[...]).astype(BF16)
    o_ref[...] = jnp.dot(mn, w_ref[...], preferred_element_type=F32)


def _memkv(mem, gain, w_bf):
    bsz, m, d = mem.shape
    n = w_bf.shape[1]
    return pl.pallas_call(
        _memkv_kernel,
        grid=(bsz,),
        in_specs=[pl.BlockSpec((None, m, d), lambda b: (b, 0, 0)),
                  pl.BlockSpec((1, d), lambda b: (0, 0)),
                  pl.BlockSpec((d, n), lambda b: (0, 0))],
        out_specs=pl.BlockSpec((None, m, n), lambda b: (b, 0, 0)),
        out_shape=jax.ShapeDtypeStruct((bsz, m, n), F32),
        compiler_params=_cparams(1),
    )(mem, gain, w_bf)


def _merge_kernel(x_ref, ret_ref, ssm_ref, qm_ref, g0, g1, g2, g3, g4, g5, kv_ref,
                  wr_ref, ws_ref, wm_ref, wo_ref, o_ref):
    kv = kv_ref[...]
    q = qm_ref[...]
    heads = []
    for h in range(MEM_HEADS):
        sl = slice(h * MEM_HEAD_DIM, (h + 1) * MEM_HEAD_DIM)
        kh = kv[:, sl].astype(BF16)
        vh = kv[:, MEM_WIDTH + h * MEM_HEAD_DIM:MEM_WIDTH + (h + 1) * MEM_HEAD_DIM].astype(BF16)
        s = lax.dot_general(q[:, sl].astype(BF16), kh, (((1,), (1,)), ((), ())),
                            preferred_element_type=F32) * (MEM_HEAD_DIM ** -0.5)
        s = s - jnp.max(s, axis=-1, keepdims=True)
        p = jnp.exp(s)
        p = p / jnp.sum(p, axis=-1, keepdims=True)
        heads.append(jnp.dot(p.astype(BF16), vh, preferred_element_type=F32))
    mattn = jnp.concatenate(heads, axis=-1)

    pr = jnp.dot(ret_ref[...].astype(BF16), wr_ref[...], preferred_element_type=F32)
    ps = jnp.dot(ssm_ref[...].astype(BF16), ws_ref[...], preferred_element_type=F32)
    pm = jnp.dot(mattn.astype(BF16), wm_ref[...], preferred_element_type=F32)
    gr = jnp.concatenate([g0[...], g1[...]], axis=-1).astype(F32)
    gs = jnp.concatenate([g2[...], g3[...]], axis=-1).astype(F32)
    gm = jnp.concatenate([g4[...], g5[...]], axis=-1).astype(F32)
    merged = _sigmoid(gr) * pr + _sigmoid(gs) * ps + _sigmoid(gm) * pm
    o_ref[...] = x_ref[...] + jnp.dot(merged.astype(BF16), wo_ref[...], preferred_element_type=F32)


def _merge(x, ret, ssm, z3, kv, wr, ws, wm, wo):
    bsz, seq, d = x.shape
    tm = 1024
    gw = 512
    gate = lambda k: pl.BlockSpec((None, tm, gw), lambda b, i, k=k: (b, i, COL_GATES // gw + k))
    full = lambda a: pl.BlockSpec(a.shape, lambda b, i: (0,) * a.ndim)
    return pl.pallas_call(
        _merge_kernel,
        grid=(bsz, seq // tm),
        in_specs=[pl.BlockSpec((None, tm, d), lambda b, i: (b, i, 0)),
                  pl.BlockSpec((None, tm, RET_V_WIDTH), lambda b, i: (b, i, 0)),
                  pl.BlockSpec((None, tm, S5_WIDTH), lambda b, i: (b, i, 0)),
                  pl.BlockSpec((None, tm, MEM_WIDTH), lambda b, i: (b, i, COL_QM // MEM_WIDTH)),
                  gate(0), gate(1), gate(2), gate(3), gate(4), gate(5),
                  pl.BlockSpec((None,) + kv.shape[1:], lambda b, i: (b, 0, 0)),
                  full(wr), full(ws), full(wm), full(wo)],
        out_specs=pl.BlockSpec((None, tm, d), lambda b, i: (b, i, 0)),
        out_shape=jax.ShapeDtypeStruct((bsz, seq, d), F32),
        compiler_params=_cparams(2),
    )(x, ret, ssm, z3, z3, z3, z3, z3, z3, z3, kv, wr, ws, wm, wo)


NEG_INF = float("-inf")
PEER_CAND = [(i, j) for i in range(PEER_TOPK) for j in range(PEER_TOPK) if (i + 1) * (j + 1) <= PEER_TOPK]


def _bitonic_merge_desc(v):
    v = list(v)
    j = len(v) // 2
    while j >= 1:
        for i in range(len(v)):
            l = i ^ j
            if l > i:
                v[i], v[l] = jnp.maximum(v[i], v[l]), jnp.minimum(v[i], v[l])
        j //= 2
    return v


def _sort_desc(v):
    v = list(v)
    n = len(v)
    k = 2
    while k <= n:
        j = k // 2
        while j >= 1:
            for i in range(n):
                l = i ^ j
                if l > i:
                    hi, lo = jnp.maximum(v[i], v[l]), jnp.minimum(v[i], v[l])
                    v[i], v[l] = (hi, lo) if (i & k) == 0 else (lo, hi)
            j //= 2
        k *= 2
    return v


def _peer_route_chunk(tc, s1t_ref, s2_ref, c1t_ref, n1t_ref, r2_ref, e2_ref, a_ref, b_ref):
    k = PEER_TOPK
    lanes = pl.ds(pl.multiple_of(tc * LANES, LANES), LANES)
    top = _sort_desc([s1t_ref[i, :, lanes] for i in range(k)])
    for g in range(1, PEER_NKEYS // k):
        grp = _sort_desc([s1t_ref[g * k + i, :, lanes] for i in range(k)])
        top = _bitonic_merge_desc([jnp.maximum(top[i], grp[k - 1 - i]) for i in range(k)])
    for r in range(k):
        a_ref[r, :, lanes] = top[r]
    for h in range(PEER_HEADS):
        s = s2_ref[h, :, lanes]
        rank = jnp.full(s.shape, float(k), F32)
        for r in range(k):
            m = jnp.max(s, axis=0, keepdims=True)
            b_ref[r, pl.ds(h, 1), lanes] = m
            hit = s == m
            rank = jnp.where(hit, float(r), rank)
            s = jnp.where(hit, NEG_INF, s)
        r2_ref[h, tc] = rank.astype(BF16)
    a = [a_ref[r, :, lanes] for r in range(k)]
    b = [b_ref[r, :, lanes] for r in range(k)]
    ea = [jnp.exp(v - a[0]) for v in a]
    eb = [jnp.exp(v - b[0]) for v in b]
    cand = [a[i] + b[j] for i, j in PEER_CAND]
    work = cand
    for r in range(k):
        tau = functools.reduce(jnp.maximum, work)
        if r + 1 < k:
            work = [jnp.where(w == tau, NEG_INF, w) for w in work]
    zsum = functools.reduce(
        jnp.add, [jnp.where(c >= tau, ea[i] * eb[j], 0.0) for c, (i, j) in zip(cand, PEER_CAND)])
    cnt = [functools.reduce(jnp.add, [jnp.where(a[i] + b[j] >= tau, 1.0, 0.0) for j in range(k)]) for i in range(k)]
    inv_z = 1.0 / zsum

    def group(g, carry):
        rows = pl.ds(pl.multiple_of(g * SUBLANES, SUBLANES), SUBLANES)
        s1 = s1t_ref[rows, :, lanes]
        n1 = jnp.zeros(s1.shape, F32)
        for i in range(k):
            n1 = jnp.where(s1 == a[i][None], cnt[i][None], n1)
        n1t_ref[rows, :, lanes] = n1
        c1t_ref[rows, :, lanes] = jnp.exp(s1 - a[0][None]) * inv_z[None]
        return carry

    lax.fori_loop(0, PEER_NKEYS // SUBLANES, group, 0)
    for h in range(PEER_HEADS):
        e2_ref[h, tc] = jnp.exp(s2_ref[h, :, lanes] - b[0][h:h + 1, :]).astype(BF16)


BF16_ROWS = 16


def _gelu_bf16(x):
    c = math.sqrt(2.0 / math.pi)
    u = x * (c + (c * 0.044715) * (x * x))
    return (0.5 * x) * (1.0 + jnp.tanh(u))


def _peer_elementwise(hid_ref, at_ref, n1s_ref, c1s_ref, r2_ref, e2_ref, *, tb, eb):
    n_slab = PEER_NKEYS // BF16_ROWS
    for r in range(eb // PEER_NKEYS):
        for tc in range(tb // LANES):
            lanes = pl.ds(tc * LANES, LANES)
            w = [None] * n_slab
            for h in range(PEER_HEADS):
                nb = jnp.broadcast_to(n1s_ref[r, pl.ds(h, 1), lanes], (BF16_ROWS, LANES)).astype(BF16)
                cb = jnp.broadcast_to(c1s_ref[r, pl.ds(h, 1), lanes], (BF16_ROWS, LANES)).astype(BF16)
                for k in range(n_slab):
                    rows = pl.ds(k * BF16_ROWS, BF16_ROWS)
                    contrib = jnp.where(r2_ref[h, tc, rows, :] < nb, e2_ref[h, tc, rows, :] * cb, 0)
                    w[k] = contrib if w[k] is None else w[k] + contrib
            for k in range(n_slab):
                rows = pl.ds(r * PEER_NKEYS + k * BF16_ROWS, BF16_ROWS)
                at_ref[tc, rows, :] = _gelu_bf16(hid_ref[tc, rows, :]) * w[k]


def _peer_kernel(h1_ref, gf_ref, wq1t_ref, wq2t_ref, kperm_ref, keys2_ref, u_ref, vt_ref, gfin_ref, y_ref,
                 xnt_ref, s1t_ref, c1t_ref, n1t_ref, s2_ref, r2_ref, e2_ref, a_ref, b_ref,
                 hid_ref, at_ref, acc_ref, n1s_ref, c1s_ref, *, tb, eb):
    s = pl.program_id(1)
    rows_per_cell = eb // PEER_NKEYS
    n_tc = tb // LANES

    @pl.when(s == 0)
    def _():
        xn = _rms(h1_ref[...], gf_ref[...])
        xnt_ref[...] = xn.T.astype(BF16)
        q1t = jnp.dot(wq1t_ref[...], xnt_ref[...], preferred_element_type=F32).astype(BF16)
        q2t = jnp.dot(wq2t_ref[...], xnt_ref[...], preferred_element_type=F32).astype(BF16)
        s1t = jnp.dot(kperm_ref[...], q1t, preferred_element_type=F32)
        s1t_ref[...] = s1t.reshape(PEER_NKEYS, PEER_HEADS, tb)
        for h in range(PEER_HEADS):
            s2_ref[h] = jnp.dot(keys2_ref[h], q2t[h * PEER_HALF:(h + 1) * PEER_HALF],
                                preferred_element_type=F32)
        def route(tc, carry):
            _peer_route_chunk(tc, s1t_ref, s2_ref, c1t_ref, n1t_ref, r2_ref, e2_ref, a_ref, b_ref)
            return carry

        lax.fori_loop(0, tb // LANES, route, 0)
        acc_ref[...] = jnp.zeros_like(acc_ref)

    first = pl.multiple_of(s * rows_per_cell, rows_per_cell)
    n1s_ref[...] = n1t_ref[pl.ds(first, rows_per_cell)]
    c1s_ref[...] = c1t_ref[pl.ds(first, rows_per_cell)]
    m_chunk = 256
    for m0 in range(0, eb, m_chunk):
        hid = jnp.dot(u_ref[m0:m0 + m_chunk, :], xnt_ref[...], preferred_element_type=F32).astype(BF16)
        for tc in range(n_tc):
            hid_ref[tc, m0:m0 + m_chunk, :] = hid[:, tc * LANES:(tc + 1) * LANES]
    _peer_elementwise(hid_ref, at_ref, n1s_ref, c1s_ref, r2_ref, e2_ref, tb=tb, eb=eb)
    at_full = jnp.concatenate([at_ref[tc] for tc in range(n_tc)], axis=1)
    acc_ref[...] += jnp.dot(vt_ref[...], at_full, preferred_element_type=F32)

    @pl.when(s == pl.num_programs(1) - 1)
    def _():
        h2 = h1_ref[...] + acc_ref[...].T
        y_ref[...] = _rms(h2, gfin_ref[...])


def _peer(h1, g_ffn, wq1t_bf, wq2t_bf, kperm_bf, keys2_bf, u_bf, vt_bf, g_final):
    t, d = h1.shape
    n_exp = u_bf.shape[0]
    tb, eb = 512, 2048
    full = lambda a: pl.BlockSpec(a.shape, lambda i, s: (0,) * a.ndim)
    route_t = pltpu.VMEM((PEER_NKEYS, PEER_HEADS, tb), F32)
    ranks = pltpu.VMEM((PEER_TOPK, PEER_HEADS, tb), F32)

    def chunked(*lead):
        return pltpu.VMEM(lead[:-1] + (tb // LANES, lead[-1], LANES), BF16)

    return pl.pallas_call(
        functools.partial(_peer_kernel, tb=tb, eb=eb),
        grid=(t // tb, n_exp // eb),
        in_specs=[pl.BlockSpec((tb, d), lambda i, s: (i, 0), pipeline_mode=pl.Buffered(1)),
                  full(g_ffn), full(wq1t_bf), full(wq2t_bf), full(kperm_bf), full(keys2_bf),
                  pl.BlockSpec((eb, d), lambda i, s: (s, 0)),
                  pl.BlockSpec((d, eb), lambda i, s: (0, s)),
                  full(g_final)],
        out_specs=pl.BlockSpec((tb, d), lambda i, s: (i, 0), pipeline_mode=pl.Buffered(1)),
        out_shape=jax.ShapeDtypeStruct((t, d), F32),
        scratch_shapes=[pltpu.VMEM((d, tb), BF16),
                        route_t,
                        route_t,
                        route_t,
                        pltpu.VMEM((PEER_HEADS, PEER_NKEYS, tb), F32),
                        chunked(PEER_HEADS, PEER_NKEYS),
                        chunked(PEER_HEADS, PEER_NKEYS),
                        ranks, ranks,
                        chunked(eb),
                        chunked(eb),
                        pltpu.VMEM((d, tb), F32),
                        pltpu.VMEM((eb // PEER_NKEYS, PEER_HEADS, tb), F32),
                        pltpu.VMEM((eb // PEER_NKEYS, PEER_HEADS, tb), F32)],
        compiler_params=pltpu.CompilerParams(dimension_semantics=("arbitrary", "arbitrary"),
                                             vmem_limit_bytes=PEER_VMEM_LIMIT),
    )(h1, g_ffn, wq1t_bf, wq2t_bf, kperm_bf, keys2_bf, u_bf, vt_bf, g_final)


def _block_diag(blocks):
    g, r, c = blocks.shape
    eye = jnp.eye(g, dtype=blocks.dtype)
    return jnp.einsum("grc,gk->grkc", blocks, eye).reshape(g * r, g * c)


def _peer_params(w_q, keys):
    d = w_q.shape[0]
    wq = w_q.reshape(d, PEER_HEADS, 2, PEER_HALF)
    wq1t = wq[:, :, 0, :].reshape(d, PEER_HEADS * PEER_HALF).T
    wq2t = wq[:, :, 1, :].reshape(d, PEER_HEADS * PEER_HALF).T
    eye = jnp.eye(PEER_HEADS, dtype=keys.dtype)
    kperm = jnp.einsum("hid,hk->ihkd", keys[:, 0], eye).reshape(PEER_NKEYS * PEER_HEADS, PEER_HEADS * PEER_HALF)
    return wq1t.astype(BF16), wq2t.astype(BF16), kperm.astype(BF16), keys[:, 1].astype(BF16)


def kernel(x, mem, positions, norm_mix, w_in, norm_mem, w_mem_kv, s5_a_re, s5_a_im, s5_log_dt, s5_b_re, s5_b_im, s5_c_re, s5_c_im, s5_d, s5_w_glu, w_branch_ret, w_branch_s5, w_branch_mem, w_out, norm_ffn, peer_w_q, peer_keys, peer_u, peer_v, norm_final):
    bsz, seq, d = x.shape
    depth = w_in.shape[0]
    assert depth == 1, "the final RMSNorm is fused into the PEER call of a single layer"
    assert bsz % SUBLANES == 0 and seq % 512 == 0 and d == 1024
    half = RET_DK // 2
    inv_freq = (ROPE_BASE ** (-jnp.arange(half, dtype=F32) / half)).reshape(1, half)
    posf = positions.astype(F32).reshape(bsz, seq, 1)
    row = lambda v: v.reshape(1, -1).astype(F32)

    h = x
    for l in range(depth):
        lam_re, lam_im, bb_re, bb_im = _s5_discretize(s5_a_re[l], s5_a_im[l], s5_log_dt[l], s5_b_re[l], s5_b_im[l])
        bmat = jnp.concatenate([_block_diag(bb_re), _block_diag(bb_im)], axis=1).astype(BF16)
        lam = jnp.stack([lam_re.reshape(-1), lam_im.reshape(-1)])
        c_re_t = jnp.transpose(s5_c_re[l], (0, 2, 1))
        c_im_t = jnp.transpose(s5_c_im[l], (0, 2, 1))
        cmat = jnp.concatenate([_block_diag(c_re_t), -_block_diag(c_im_t)], axis=0).astype(BF16)

        z = _inproj(h.reshape(bsz * seq, d), row(norm_mix[l]), w_in[l].astype(BF16))
        z3 = z.reshape(bsz, seq, IN_COLS)
        ret = _retention(z3, posf, inv_freq)
        u_sb = jnp.transpose(z3[:, :, COL_U:COL_U + S5_WIDTH], (1, 0, 2)).astype(F32)
        ssm_sb = _s5(u_sb, bmat, lam, cmat, row(s5_d[l]), s5_w_glu[l].astype(BF16))
        ssm = jnp.transpose(ssm_sb, (1, 0, 2))
        kv = _memkv(mem, row(norm_mem[l]), w_mem_kv[l].astype(BF16))
        h1 = _merge(h, ret, ssm, z3, kv, w_branch_ret[l].astype(BF16), w_branch_s5[l].astype(BF16),
                    w_branch_mem[l].astype(BF16), w_out[l].astype(BF16))
        wq1t, wq2t, kperm, keys2 = _peer_params(peer_w_q[l], peer_keys[l])
        y = _peer(h1.reshape(bsz * seq, d), row(norm_ffn[l]), wq1t, wq2t, kperm, keys2,
                  peer_u[l].astype(BF16), peer_v[l].T.astype(BF16), row(norm_final))
        h = y.reshape(bsz, seq, d)
    return h
```

```python
import functools
import math

import jax
import jax.numpy as jnp
from jax import lax
from jax.experimental import pallas as pl
from jax.experimental.pallas import tpu as pltpu

F32 = jnp.float32
BF16 = jnp.bfloat16

EPS = 1e-6
RET_HEADS = 4
RET_DK = 128
RET_DV = 128
RET_QK_WIDTH = RET_HEADS * RET_DK
RET_V_WIDTH = RET_HEADS * RET_DV
ROPE_BASE = 10000.0
S5_GROUPS = 16
S5_GROUP_DIM = 16
S5_STATE = 64
S5_WIDTH = S5_GROUPS * S5_GROUP_DIM
S5_NSTATE = S5_GROUPS * S5_STATE
MEM_HEADS = 4
MEM_HEAD_DIM = 64
MEM_WIDTH = MEM_HEADS * MEM_HEAD_DIM
PEER_HEADS = 8
PEER_NKEYS = 128
PEER_HALF = 128
PEER_TOPK = 16

LANES = 128
SUBLANES = 8
VMEM_LIMIT = 48 * 1024 * 1024
PEER_VMEM_LIMIT = 60 * 1024 * 1024

COL_Q, COL_K, COL_V, COL_G = 0, 512, 1024, 1536
COL_U, COL_QM, COL_GATES = 2048, 2304, 2560
IN_COLS = 5632


def _cparams(n_axes):
    return pltpu.CompilerParams(dimension_semantics=("arbitrary",) * n_axes, vmem_limit_bytes=VMEM_LIMIT)


def _rms(x, gain):
    return x * lax.rsqrt(jnp.mean(x * x, axis=-1, keepdims=True) + EPS) * gain


def _gelu(x):
    c = math.sqrt(2.0 / math.pi)
    return 0.5 * x * (1.0 + jnp.tanh(c * (x + 0.044715 * (x * x * x))))


def _sigmoid(x):
    return 1.0 / (1.0 + jnp.exp(-x))


def _s5_disc_kernel(are_ref, aim_ref, ldt_ref, bre_ref, bim_ref, lre_ref, lim_ref, bbre_ref, bbim_ref):
    a_re = are_ref[...]
    a_im = aim_ref[...]
    dt = jnp.exp(ldt_ref[...])
    mag = jnp.exp(a_re * dt)
    lam_re = mag * jnp.cos(a_im * dt)
    lam_im = mag * jnp.sin(a_im * dt)
    den = a_re * a_re + a_im * a_im
    nr = lam_re - 1.0
    ni = lam_im
    cr = (nr * a_re + ni * a_im) / den
    ci = (ni * a_re - nr * a_im) / den
    b_re = bre_ref[...]
    b_im = bim_ref[...]
    lre_ref[...] = lam_re
    lim_ref[...] = lam_im
    bbre_ref[...] = cr * b_re - ci * b_im
    bbim_ref[...] = cr * b_im + ci * b_re


def _s5_discretize(a_re, a_im, log_dt, b_re, b_im):
    g, p = a_re.shape
    h = b_re.shape[-1]
    rep = lambda a: jnp.repeat(a, h, axis=0)
    b_re_t = jnp.transpose(b_re, (0, 2, 1)).reshape(g * h, p)
    b_im_t = jnp.transpose(b_im, (0, 2, 1)).reshape(g * h, p)
    shp = jax.ShapeDtypeStruct((g * h, p), F32)
    lam_re, lam_im, bb_re, bb_im = pl.pallas_call(
        _s5_disc_kernel, out_shape=(shp, shp, shp, shp),
    )(rep(a_re), rep(a_im), rep(log_dt[:, None]), b_re_t, b_im_t)
    return lam_re[::h], lam_im[::h], bb_re.reshape(g, h, p), bb_im.reshape(g, h, p)


def _inproj_kernel(x_ref, g_ref, w_ref, z_ref, xn_ref):
    @pl.when(pl.program_id(1) == 0)
    def _():
        xn_ref[...] = _rms(x_ref[...], g_ref[...]).astype(BF16)

    z_ref[...] = jnp.dot(xn_ref[...], w_ref[...], preferred_element_type=F32).astype(BF16)


def _inproj(x2, gain, w_bf):
    t, d = x2.shape
    tm, tn = 512, IN_COLS
    return pl.pallas_call(
        _inproj_kernel,
        grid=(t // tm, IN_COLS // tn),
        in_specs=[pl.BlockSpec((tm, d), lambda i, j: (i, 0)),
                  pl.BlockSpec((1, d), lambda i, j: (0, 0)),
                  pl.BlockSpec((d, tn), lambda i, j: (0, j), pipeline_mode=pl.Buffered(1))],
        out_specs=pl.BlockSpec((tm, tn), lambda i, j: (i, j)),
        out_shape=jax.ShapeDtypeStruct((t, IN_COLS), BF16),
        scratch_shapes=[pltpu.VMEM((tm, d), BF16)],
        compiler_params=_cparams(2),
    )(x2, gain, w_bf)


def _ret_log_gamma(h):
    return math.log1p(-(2.0 ** (-5.0 - h)))


def _retention_kernel(q_ref, k_ref, v_ref, g_ref, pos_ref, invf_ref, o_ref,
                      state_ref, dmat_ref, qdec_ref, kdec_ref, *, lc):
    b = pl.program_id(0)
    c = pl.program_id(1)

    @pl.when((b == 0) & (c == 0))
    def _():
        row = lax.broadcasted_iota(jnp.int32, (lc, lc), 0)
        col = lax.broadcasted_iota(jnp.int32, (lc, lc), 1)
        rel = (row - col).astype(F32)
        pos = lax.broadcasted_iota(jnp.int32, (lc, RET_DK), 0).astype(F32)
        for h in range(RET_HEADS):
            lg = _ret_log_gamma(h)
            dmat_ref[h] = jnp.where(rel >= 0.0, jnp.exp(rel * lg), 0.0)
            qdec_ref[h] = jnp.exp((pos + 1.0) * lg)
            kdec_ref[h] = jnp.exp((lc - 1.0 - pos) * lg)

    @pl.when(c == 0)
    def _():
        state_ref[...] = jnp.zeros_like(state_ref)

    ang = pos_ref[...] * invf_ref[...]
    cs = jnp.cos(ang)
    sn = jnp.sin(ang)
    cc = jnp.concatenate([cs, cs], axis=-1)
    ss = jnp.concatenate([-sn, sn], axis=-1)
    half = RET_DK // 2
    for h in range(RET_HEADS):
        sl = slice(h * RET_DK, (h + 1) * RET_DK)
        qh = q_ref[:, sl].astype(F32)
        kh = k_ref[:, sl].astype(F32)
        qr = qh * cc + pltpu.roll(qh, half, 1) * ss
        kr = (kh * cc + pltpu.roll(kh, half, 1) * ss) * (RET_DK ** -0.5)
        vb = v_ref[:, sl]
        scores = lax.dot_general(qr.astype(BF16), kr.astype(BF16), (((1,), (1,)), ((), ())),
                                 preferred_element_type=F32) * dmat_ref[h]
        inner = jnp.dot(scores.astype(BF16), vb, preferred_element_type=F32)
        st = state_ref[h]
        cross = jnp.dot((qr * qdec_ref[h]).astype(BF16), st.astype(BF16), preferred_element_type=F32)
        kdt = (kr * kdec_ref[h]).T.astype(BF16)
        state_ref[h] = st * math.exp(lc * _ret_log_gamma(h)) + jnp.dot(kdt, vb, preferred_element_type=F32)
        o = inner + cross
        o = o * lax.rsqrt(jnp.mean(o * o, axis=-1, keepdims=True) + EPS)
        gh = g_ref[:, sl].astype(F32)
        o_ref[:, sl] = o * (gh * _sigmoid(gh))


def _retention(z3, posf, inv_freq):
    bsz, seq, _ = z3.shape
    lc = 256
    blk = lambda k: pl.BlockSpec((None, lc, RET_QK_WIDTH), lambda b, c, k=k: (b, c, k))
    return pl.pallas_call(
        functools.partial(_retention_kernel, lc=lc),
        grid=(bsz, seq // lc),
        in_specs=[blk(COL_Q // 512), blk(COL_K // 512), blk(COL_V // 512), blk(COL_G // 512),
                  pl.BlockSpec((None, lc, 1), lambda b, c: (b, c, 0)),
                  pl.BlockSpec((1, RET_DK // 2), lambda b, c: (0, 0))],
        out_specs=pl.BlockSpec((None, lc, RET_V_WIDTH), lambda b, c: (b, c, 0)),
        out_shape=jax.ShapeDtypeStruct((bsz, seq, RET_V_WIDTH), F32),
        scratch_shapes=[pltpu.VMEM((RET_HEADS, RET_DK, RET_DV), F32),
                        pltpu.VMEM((RET_HEADS, lc, lc), F32),
                        pltpu.VMEM((RET_HEADS, lc, RET_DK), F32),
                        pltpu.VMEM((RET_HEADS, lc, RET_DK), F32)],
        compiler_params=_cparams(2),
    )(z3, z3, z3, z3, posf, inv_freq)


S5_SCAN_COLS = 512


def _s5_kernel(u_ref, bmat_ref, lam_ref, cmat_ref, d_ref, wglu_ref, o_ref, x_ref, state_ref, *, ts, bsz):
    @pl.when(pl.program_id(0) == 0)
    def _():
        state_ref[...] = jnp.zeros_like(state_ref)

    u = u_ref[...].reshape(ts * bsz, S5_WIDTH)
    x_ref[...] = jnp.dot(u.astype(BF16), bmat_ref[...], preferred_element_type=F32)

    for c0 in range(0, S5_NSTATE, S5_SCAN_COLS):
        re = pl.ds(c0, S5_SCAN_COLS)
        im = pl.ds(S5_NSTATE + c0, S5_SCAN_COLS)
        lr = jnp.broadcast_to(lam_ref[0:1, c0:c0 + S5_SCAN_COLS], (bsz, S5_SCAN_COLS))
        li = jnp.broadcast_to(lam_ref[1:2, c0:c0 + S5_SCAN_COLS], (bsz, S5_SCAN_COLS))

        def step(t, carry):
            xr, xi = carry
            rows = pl.ds(pl.multiple_of(t * bsz, bsz), bsz)
            nr = lr * xr - li * xi + x_ref[rows, re]
            ni = lr * xi + li * xr + x_ref[rows, im]
            x_ref[rows, re] = nr
            x_ref[rows, im] = ni
            return nr, ni

        xr, xi = lax.fori_loop(0, ts, step, (state_ref[:, re], state_ref[:, im]), unroll=8)
        state_ref[:, re] = xr
        state_ref[:, im] = xi

    y = jnp.dot(x_ref[...].astype(BF16), cmat_ref[...], preferred_element_type=F32) + d_ref[...] * u
    y = _gelu(y)
    lg = jnp.dot(y.astype(BF16), wglu_ref[...], preferred_element_type=F32)
    out = lg[:, :S5_WIDTH] * _sigmoid(lg[:, S5_WIDTH:])
    o_ref[...] = out.reshape(ts, bsz, S5_WIDTH)


def _s5(u_sb, bmat, lam, cmat, d, wglu_bf):
    seq, bsz, _ = u_sb.shape
    ts = 256
    full = lambda a: pl.BlockSpec(a.shape, lambda i: (0,) * a.ndim)
    return pl.pallas_call(
        functools.partial(_s5_kernel, ts=ts, bsz=bsz),
        grid=(seq // ts,),
        in_specs=[pl.BlockSpec((ts, bsz, S5_WIDTH), lambda i: (i, 0, 0)),
                  full(bmat), full(lam), full(cmat), full(d), full(wglu_bf)],
        out_specs=pl.BlockSpec((ts, bsz, S5_WIDTH), lambda i: (i, 0, 0)),
        out_shape=jax.ShapeDtypeStruct((seq, bsz, S5_WIDTH), F32),
        scratch_shapes=[pltpu.VMEM((ts * bsz, 2 * S5_NSTATE), F32),
                        pltpu.VMEM((bsz, 2 * S5_NSTATE), F32)],
        compiler_params=_cparams(1),
    )(u_sb, bmat, lam, cmat, d, wglu_bf)


def _memkv_kernel(m_ref, g_ref, w_ref, o_ref):
    mn = _rms(m_ref[...], g_ref[...]).astype(BF16)
    o_ref[...] = jnp.dot(mn, w_ref[...], preferred_element_type=F32)


def _memkv(mem, gain, w_bf):
    bsz, m, d = mem.shape
    n = w_bf.shape[1]
    return pl.pallas_call(
        _memkv_kernel,
        grid=(bsz,),
        in_specs=[pl.BlockSpec((None, m, d), lambda b: (b, 0, 0)),
                  pl.BlockSpec((1, d), lambda b: (0, 0)),
                  pl.BlockSpec((d, n), lambda b: (0, 0))],
        out_specs=pl.BlockSpec((None, m, n), lambda b: (b, 0, 0)),
        out_shape=jax.ShapeDtypeStruct((bsz, m, n), F32),
        compiler_params=_cparams(1),
    )(mem, gain, w_bf)


def _merge_kernel(x_ref, ret_ref, ssm_ref, qm_ref, g0, g1, g2, g3, g4, g5, kv_ref,
                  wr_ref, ws_ref, wm_ref, wo_ref, o_ref):
    kv = kv_ref[...]
    q = qm_ref[...]
    heads = []
    for h in range(MEM_HEADS):
        sl = slice(h * MEM_HEAD_DIM, (h + 1) * MEM_HEAD_DIM)
        kh = kv[:, sl].astype(BF16)
        vh = kv[:, MEM_WIDTH + h * MEM_HEAD_DIM:MEM_WIDTH + (h + 1) * MEM_HEAD_DIM].astype(BF16)
        s = lax.dot_general(q[:, sl].astype(BF16), kh, (((1,), (1,)), ((), ())),
                            preferred_element_type=F32) * (MEM_HEAD_DIM ** -0.5)
        s = s - jnp.max(s, axis=-1, keepdims=True)
        p = jnp.exp(s)
        p = p / jnp.sum(p, axis=-1, keepdims=True)
        heads.append(jnp.dot(p.astype(BF16), vh, preferred_element_type=F32))
    mattn = jnp.concatenate(heads, axis=-1)

    pr = jnp.dot(ret_ref[...].astype(BF16), wr_ref[...], preferred_element_type=F32)
    ps = jnp.dot(ssm_ref[...].astype(BF16), ws_ref[...], preferred_element_type=F32)
    pm = jnp.dot(mattn.astype(BF16), wm_ref[...], preferred_element_type=F32)
    gr = jnp.concatenate([g0[...], g1[...]], axis=-1).astype(F32)
    gs = jnp.concatenate([g2[...], g3[...]], axis=-1).astype(F32)
    gm = jnp.concatenate([g4[...], g5[...]], axis=-1).astype(F32)
    merged = _sigmoid(gr) * pr + _sigmoid(gs) * ps + _sigmoid(gm) * pm
    o_ref[...] = x_ref[...] + jnp.dot(merged.astype(BF16), wo_ref[...], preferred_element_type=F32)


def _merge(x, ret, ssm, z3, kv, wr, ws, wm, wo):
    bsz, seq, d = x.shape
    tm = 1024
    gw = 512
    gate = lambda k: pl.BlockSpec((None, tm, gw), lambda b, i, k=k: (b, i, COL_GATES // gw + k))
    full = lambda a: pl.BlockSpec(a.shape, lambda b, i: (0,) * a.ndim)
    return pl.pallas_call(
        _merge_kernel,
        grid=(bsz, seq // tm),
        in_specs=[pl.BlockSpec((None, tm, d), lambda b, i: (b, i, 0)),
                  pl.BlockSpec((None, tm, RET_V_WIDTH), lambda b, i: (b, i, 0)),
                  pl.BlockSpec((None, tm, S5_WIDTH), lambda b, i: (b, i, 0)),
                  pl.BlockSpec((None, tm, MEM_WIDTH), lambda b, i: (b, i, COL_QM // MEM_WIDTH)),
                  gate(0), gate(1), gate(2), gate(3), gate(4), gate(5),
                  pl.BlockSpec((None,) + kv.shape[1:], lambda b, i: (b, 0, 0)),
                  full(wr), full(ws), full(wm), full(wo)],
        out_specs=pl.BlockSpec((None, tm, d), lambda b, i: (b, i, 0)),
        out_shape=jax.ShapeDtypeStruct((bsz, seq, d), F32),
        compiler_params=_cparams(2),
    )(x, ret, ssm, z3, z3, z3, z3, z3, z3, z3, kv, wr, ws, wm, wo)


NEG_INF = float("-inf")
PEER_CAND = [(i, j) for i in range(PEER_TOPK) for j in range(PEER_TOPK) if (i + 1) * (j + 1) <= PEER_TOPK]


def _bitonic_merge_desc(v):
    v = list(v)
    j = len(v) // 2
    while j >= 1:
        for i in range(len(v)):
            l = i ^ j
            if l > i:
                v[i], v[l] = jnp.maximum(v[i], v[l]), jnp.minimum(v[i], v[l])
        j //= 2
    return v


def _sort_desc(v):
    v = list(v)
    n = len(v)
    k = 2
    while k <= n:
        j = k // 2
        while j >= 1:
            for i in range(n):
                l = i ^ j
                if l > i:
                    hi, lo = jnp.maximum(v[i], v[l]), jnp.minimum(v[i], v[l])
                    v[i], v[l] = (hi, lo) if (i & k) == 0 else (lo, hi)
            j //= 2
        k *= 2
    return v


def _peer_route_chunk(tc, s1t_ref, s2_ref, c1t_ref, n1t_ref, r2_ref, e2_ref, a_ref, b_ref):
    k = PEER_TOPK
    lanes = pl.ds(pl.multiple_of(tc * LANES, LANES), LANES)
    top = _sort_desc([s1t_ref[i, :, lanes] for i in range(k)])
    for g in range(1, PEER_NKEYS // k):
        grp = _sort_desc([s1t_ref[g * k + i, :, lanes] for i in range(k)])
        top = _bitonic_merge_desc([jnp.maximum(top[i], grp[k - 1 - i]) for i in range(k)])
    for r in range(k):
        a_ref[r, :, lanes] = top[r]
    for h in range(PEER_HEADS):
        s = s2_ref[h, :, lanes]
        rank = jnp.full(s.shape, float(k), F32)
        for r in range(k):
            m = jnp.max(s, axis=0, keepdims=True)
            b_ref[r, pl.ds(h, 1), lanes] = m
            hit = s == m
            rank = jnp.where(hit, float(r), rank)
            s = jnp.where(hit, NEG_INF, s)
        r2_ref[h, tc] = rank.astype(BF16)
    a = [a_ref[r, :, lanes] for r in range(k)]
    b = [b_ref[r, :, lanes] for r in range(k)]
    ea = [jnp.exp(v - a[0]) for v in a]
    eb = [jnp.exp(v - b[0]) for v in b]
    cand = [a[i] + b[j] for i, j in PEER_CAND]
    work = cand
    for r in range(k):
        tau = functools.reduce(jnp.maximum, work)
        if r + 1 < k:
            work = [jnp.where(w == tau, NEG_INF, w) for w in work]
    zsum = functools.reduce(
        jnp.add, [jnp.where(c >= tau, ea[i] * eb[j], 0.0) for c, (i, j) in zip(cand, PEER_CAND)])
    cnt = [functools.reduce(jnp.add, [jnp.where(a[i] + b[j] >= tau, 1.0, 0.0) for j in range(k)]) for i in range(k)]
    inv_z = 1.0 / zsum

    def group(g, carry):
        rows = pl.ds(pl.multiple_of(g * SUBLANES, SUBLANES), SUBLANES)
        s1 = s1t_ref[rows, :, lanes]
        n1 = jnp.zeros(s1.shape, F32)
        for i in range(k):
            n1 = jnp.where(s1 == a[i][None], cnt[i][None], n1)
        n1t_ref[rows, :, lanes] = n1
        c1t_ref[rows, :, lanes] = jnp.exp(s1 - a[0][None]) * inv_z[None]
        return carry

    lax.fori_loop(0, PEER_NKEYS // SUBLANES, group, 0)
    for h in range(PEER_HEADS):
        e2_ref[h, tc] = jnp.exp(s2_ref[h, :, lanes] - b[0][h:h + 1, :]).astype(BF16)


BF16_ROWS = 16
PEER_MM1_ROWS = 512


def _gelu_bf16(x):
    c = math.sqrt(2.0 / math.pi)
    u = x * (c + (c * 0.044715) * (x * x))
    return (0.5 * x) * (1.0 + jnp.tanh(u))


def _peer_elementwise(hid_ref, at_ref, n1s_ref, c1s_ref, r2_ref, e2_ref, *, tb, eb):
    n_slab = PEER_NKEYS // BF16_ROWS
    for r in range(eb // PEER_NKEYS):
        for tc in range(tb // LANES):
            lanes = pl.ds(tc * LANES, LANES)
            w = [None] * n_slab
            for h in range(PEER_HEADS):
                nb = jnp.broadcast_to(n1s_ref[r, pl.ds(h, 1), lanes], (BF16_ROWS, LANES)).astype(BF16)
                cb = jnp.broadcast_to(c1s_ref[r, pl.ds(h, 1), lanes], (BF16_ROWS, LANES)).astype(BF16)
                for k in range(n_slab):
                    rows = pl.ds(k * BF16_ROWS, BF16_ROWS)
                    contrib = jnp.where(r2_ref[h, tc, rows, :] < nb, e2_ref[h, tc, rows, :] * cb, 0)
                    w[k] = contrib if w[k] is None else w[k] + contrib
            for k in range(n_slab):
                rows = pl.ds(r * PEER_NKEYS + k * BF16_ROWS, BF16_ROWS)
                at_ref[tc, rows, :] = _gelu_bf16(hid_ref[tc, rows, :]) * w[k]


def _peer_kernel(h1_ref, gf_ref, wq1t_ref, wq2t_ref, kperm_ref, keys2_ref, u_ref, vt_ref, gfin_ref, y_ref,
                 xnt_ref, s1t_ref, c1t_ref, n1t_ref, s2_ref, r2_ref, e2_ref, a_ref, b_ref,
                 hid_ref, at_ref, acc_ref, n1s_ref, c1s_ref, *, tb, eb):
    s = pl.program_id(1)
    rows_per_cell = eb // PEER_NKEYS
    n_tc = tb // LANES

    @pl.when(s == 0)
    def _():
        xn = _rms(h1_ref[...], gf_ref[...])
        xnt_ref[...] = xn.T.astype(BF16)
        q1t = jnp.dot(wq1t_ref[...], xnt_ref[...], preferred_element_type=F32).astype(BF16)
        q2t = jnp.dot(wq2t_ref[...], xnt_ref[...], preferred_element_type=F32).astype(BF16)
        s1t = jnp.dot(kperm_ref[...], q1t, preferred_element_type=F32)
        s1t_ref[...] = s1t.reshape(PEER_NKEYS, PEER_HEADS, tb)
        for h in range(PEER_HEADS):
            s2_ref[h] = jnp.dot(keys2_ref[h], q2t[h * PEER_HALF:(h + 1) * PEER_HALF],
                                preferred_element_type=F32)
        def route(tc, carry):
            _peer_route_chunk(tc, s1t_ref, s2_ref, c1t_ref, n1t_ref, r2_ref, e2_ref, a_ref, b_ref)
            return carry

        lax.fori_loop(0, tb // LANES, route, 0)
        acc_ref[...] = jnp.zeros_like(acc_ref)

    first = pl.multiple_of(s * rows_per_cell, rows_per_cell)
    n1s_ref[...] = n1t_ref[pl.ds(first, rows_per_cell)]
    c1s_ref[...] = c1t_ref[pl.ds(first, rows_per_cell)]
    for m0 in range(0, eb, PEER_MM1_ROWS):
        hid = jnp.dot(u_ref[m0:m0 + PEER_MM1_ROWS, :], xnt_ref[...], preferred_element_type=F32).astype(BF16)
        for tc in range(n_tc):
            hid_ref[tc, m0:m0 + PEER_MM1_ROWS, :] = hid[:, tc * LANES:(tc + 1) * LANES]
    _peer_elementwise(hid_ref, at_ref, n1s_ref, c1s_ref, r2_ref, e2_ref, tb=tb, eb=eb)
    at_full = jnp.concatenate([at_ref[tc] for tc in range(n_tc)], axis=1)
    acc_ref[...] += jnp.dot(vt_ref[...], at_full, preferred_element_type=F32)

    @pl.when(s == pl.num_programs(1) - 1)
    def _():
        h2 = h1_ref[...] + acc_ref[...].T
        y_ref[...] = _rms(h2, gfin_ref[...])


def _peer(h1, g_ffn, wq1t_bf, wq2t_bf, kperm_bf, keys2_bf, u_bf, vt_bf, g_final):
    t, d = h1.shape
    n_exp = u_bf.shape[0]
    tb, eb = 512, 2048
    full = lambda a: pl.BlockSpec(a.shape, lambda i, s: (0,) * a.ndim)
    route_t = pltpu.VMEM((PEER_NKEYS, PEER_HEADS, tb), F32)
    ranks = pltpu.VMEM((PEER_TOPK, PEER_HEADS, tb), F32)

    def chunked(*lead):
        return pltpu.VMEM(lead[:-1] + (tb // LANES, lead[-1], LANES), BF16)

    return pl.pallas_call(
        functools.partial(_peer_kernel, tb=tb, eb=eb),
        grid=(t // tb, n_exp // eb),
        in_specs=[pl.BlockSpec((tb, d), lambda i, s: (i, 0), pipeline_mode=pl.Buffered(1)),
                  full(g_ffn), full(wq1t_bf), full(wq2t_bf), full(kperm_bf), full(keys2_bf),
                  pl.BlockSpec((eb, d), lambda i, s: (s, 0)),
                  pl.BlockSpec((d, eb), lambda i, s: (0, s)),
                  full(g_final)],
        out_specs=pl.BlockSpec((tb, d), lambda i, s: (i, 0), pipeline_mode=pl.Buffered(1)),
        out_shape=jax.ShapeDtypeStruct((t, d), F32),
        scratch_shapes=[pltpu.VMEM((d, tb), BF16),
                        route_t,
                        route_t,
                        route_t,
                        pltpu.VMEM((PEER_HEADS, PEER_NKEYS, tb), F32),
                        chunked(PEER_HEADS, PEER_NKEYS),
                        chunked(PEER_HEADS, PEER_NKEYS),
                        ranks, ranks,
                        chunked(eb),
                        chunked(eb),
                        pltpu.VMEM((d, tb), F32),
                        pltpu.VMEM((eb // PEER_NKEYS, PEER_HEADS, tb), F32),
                        pltpu.VMEM((eb // PEER_NKEYS, PEER_HEADS, tb), F32)],
        compiler_params=pltpu.CompilerParams(dimension_semantics=("arbitrary", "arbitrary"),
                                             vmem_limit_bytes=PEER_VMEM_LIMIT),
    )(h1, g_ffn, wq1t_bf, wq2t_bf, kperm_bf, keys2_bf, u_bf, vt_bf, g_final)


def _block_diag(blocks):
    g, r, c = blocks.shape
    eye = jnp.eye(g, dtype=blocks.dtype)
    return jnp.einsum("grc,gk->grkc", blocks, eye).reshape(g * r, g * c)


def _peer_params(w_q, keys):
    d = w_q.shape[0]
    wq = w_q.reshape(d, PEER_HEADS, 2, PEER_HALF)
    wq1t = wq[:, :, 0, :].reshape(d, PEER_HEADS * PEER_HALF).T
    wq2t = wq[:, :, 1, :].reshape(d, PEER_HEADS * PEER_HALF).T
    eye = jnp.eye(PEER_HEADS, dtype=keys.dtype)
    kperm = jnp.einsum("hid,hk->ihkd", keys[:, 0], eye).reshape(PEER_NKEYS * PEER_HEADS, PEER_HEADS * PEER_HALF)
    return wq1t.astype(BF16), wq2t.astype(BF16), kperm.astype(BF16), keys[:, 1].astype(BF16)


def kernel(x, mem, positions, norm_mix, w_in, norm_mem, w_mem_kv, s5_a_re, s5_a_im, s5_log_dt, s5_b_re, s5_b_im, s5_c_re, s5_c_im, s5_d, s5_w_glu, w_branch_ret, w_branch_s5, w_branch_mem, w_out, norm_ffn, peer_w_q, peer_keys, peer_u, peer_v, norm_final):
    bsz, seq, d = x.shape
    depth = w_in.shape[0]
    assert depth == 1, "the final RMSNorm is fused into the PEER call of a single layer"
    assert bsz % SUBLANES == 0 and seq % 512 == 0 and d == 1024
    half = RET_DK // 2
    inv_freq = (ROPE_BASE ** (-jnp.arange(half, dtype=F32) / half)).reshape(1, half)
    posf = positions.astype(F32).reshape(bsz, seq, 1)
    row = lambda v: v.reshape(1, -1).astype(F32)

    h = x
    for l in range(depth):
        lam_re, lam_im, bb_re, bb_im = _s5_discretize(s5_a_re[l], s5_a_im[l], s5_log_dt[l], s5_b_re[l], s5_b_im[l])
        bmat = jnp.concatenate([_block_diag(bb_re), _block_diag(bb_im)], axis=1).astype(BF16)
        lam = jnp.stack([lam_re.reshape(-1), lam_im.reshape(-1)])
        c_re_t = jnp.transpose(s5_c_re[l], (0, 2, 1))
        c_im_t = jnp.transpose(s5_c_im[l], (0, 2, 1))
        cmat = jnp.concatenate([_block_diag(c_re_t), -_block_diag(c_im_t)], axis=0).astype(BF16)

        z = _inproj(h.reshape(bsz * seq, d), row(norm_mix[l]), w_in[l].astype(BF16))
        z3 = z.reshape(bsz, seq, IN_COLS)
        ret = _retention(z3, posf, inv_freq)
        u_sb = jnp.transpose(z3[:, :, COL_U:COL_U + S5_WIDTH], (1, 0, 2)).astype(F32)
        ssm_sb = _s5(u_sb, bmat, lam, cmat, row(s5_d[l]), s5_w_glu[l].astype(BF16))
        ssm = jnp.transpose(ssm_sb, (1, 0, 2))
        kv = _memkv(mem, row(norm_mem[l]), w_mem_kv[l].astype(BF16))
        h1 = _merge(h, ret, ssm, z3, kv, w_branch_ret[l].astype(BF16), w_branch_s5[l].astype(BF16),
                    w_branch_mem[l].astype(BF16), w_out[l].astype(BF16))
        wq1t, wq2t, kperm, keys2 = _peer_params(peer_w_q[l], peer_keys[l])
        y = _peer(h1.reshape(bsz * seq, d), row(norm_ffn[l]), wq1t, wq2t, kperm, keys2,
                  peer_u[l].astype(BF16), peer_v[l].T.astype(BF16), row(norm_final))
        h = y.reshape(bsz, seq, d)
    return h
```

```python
import functools
import math

import jax
import jax.numpy as jnp
from jax import lax
from jax.experimental import pallas as pl
from jax.experimental.pallas import tpu as pltpu

F32 = jnp.float32
BF16 = jnp.bfloat16

EPS = 1e-6
RET_HEADS = 4
RET_DK = 128
RET_DV = 128
RET_QK_WIDTH = RET_HEADS * RET_DK
RET_V_WIDTH = RET_HEADS * RET_DV
ROPE_BASE = 10000.0
S5_GROUPS = 16
S5_GROUP_DIM = 16
S5_STATE = 64
S5_WIDTH = S5_GROUPS * S5_GROUP_DIM
S5_NSTATE = S5_GROUPS * S5_STATE
MEM_HEADS = 4
MEM_HEAD_DIM = 64
MEM_WIDTH = MEM_HEADS * MEM_HEAD_DIM
PEER_HEADS = 8
PEER_NKEYS = 128
PEER_HALF = 128
PEER_TOPK = 16

LANES = 128
SUBLANES = 8
VMEM_LIMIT = 48 * 1024 * 1024
PEER_VMEM_LIMIT = 60 * 1024 * 1024

COL_Q, COL_K, COL_V, COL_G = 0, 512, 1024, 1536
COL_U, COL_QM, COL_GATES = 2048, 2304, 2560
IN_COLS = 5632


def _cparams(n_axes):
    return pltpu.CompilerParams(dimension_semantics=("arbitrary",) * n_axes, vmem_limit_bytes=VMEM_LIMIT)


def _rms(x, gain):
    return x * lax.rsqrt(jnp.mean(x * x, axis=-1, keepdims=True) + EPS) * gain


def _gelu(x):
    c = math.sqrt(2.0 / math.pi)
    return 0.5 * x * (1.0 + jnp.tanh(c * (x + 0.044715 * (x * x * x))))


def _sigmoid(x):
    return 1.0 / (1.0 + jnp.exp(-x))


def _s5_disc_kernel(are_ref, aim_ref, ldt_ref, bre_ref, bim_ref, lre_ref, lim_ref, bbre_ref, bbim_ref):
    a_re = are_ref[...]
    a_im = aim_ref[...]
    dt = jnp.exp(ldt_ref[...])
    mag = jnp.exp(a_re * dt)
    lam_re = mag * jnp.cos(a_im * dt)
    lam_im = mag * jnp.sin(a_im * dt)
    den = a_re * a_re + a_im * a_im
    nr = lam_re - 1.0
    ni = lam_im
    cr = (nr * a_re + ni * a_im) / den
    ci = (ni * a_re - nr * a_im) / den
    b_re = bre_ref[...]
    b_im = bim_ref[...]
    lre_ref[...] = lam_re
    lim_ref[...] = lam_im
    bbre_ref[...] = cr * b_re - ci * b_im
    bbim_ref[...] = cr * b_im + ci * b_re


def _s5_discretize(a_re, a_im, log_dt, b_re, b_im):
    g, p = a_re.shape
    h = b_re.shape[-1]
    rep = lambda a: jnp.repeat(a, h, axis=0)
    b_re_t = jnp.transpose(b_re, (0, 2, 1)).reshape(g * h, p)
    b_im_t = jnp.transpose(b_im, (0, 2, 1)).reshape(g * h, p)
    shp = jax.ShapeDtypeStruct((g * h, p), F32)
    lam_re, lam_im, bb_re, bb_im = pl.pallas_call(
        _s5_disc_kernel, out_shape=(shp, shp, shp, shp),
    )(rep(a_re), rep(a_im), rep(log_dt[:, None]), b_re_t, b_im_t)
    return lam_re[::h], lam_im[::h], bb_re.reshape(g, h, p), bb_im.reshape(g, h, p)


def _inproj_kernel(x_ref, g_ref, w_ref, z_ref, xn_ref):
    @pl.when(pl.program_id(1) == 0)
    def _():
        xn_ref[...] = _rms(x_ref[...], g_ref[...]).astype(BF16)

    z_ref[...] = jnp.dot(xn_ref[...], w_ref[...], preferred_element_type=F32).astype(BF16)


def _inproj(x2, gain, w_bf):
    t, d = x2.shape
    tm, tn = 512, IN_COLS
    return pl.pallas_call(
        _inproj_kernel,
        grid=(t // tm, IN_COLS // tn),
        in_specs=[pl.BlockSpec((tm, d), lambda i, j: (i, 0)),
                  pl.BlockSpec((1, d), lambda i, j: (0, 0)),
                  pl.BlockSpec((d, tn), lambda i, j: (0, j), pipeline_mode=pl.Buffered(1))],
        out_specs=pl.BlockSpec((tm, tn), lambda i, j: (i, j)),
        out_shape=jax.ShapeDtypeStruct((t, IN_COLS), BF16),
        scratch_shapes=[pltpu.VMEM((tm, d), BF16)],
        compiler_params=_cparams(2),
    )(x2, gain, w_bf)


def _ret_log_gamma(h):
    return math.log1p(-(2.0 ** (-5.0 - h)))


def _retention_kernel(q_ref, k_ref, v_ref, g_ref, pos_ref, invf_ref, o_ref,
                      state_ref, dmat_ref, qdec_ref, kdec_ref, *, lc):
    b = pl.program_id(0)
    c = pl.program_id(1)

    @pl.when((b == 0) & (c == 0))
    def _():
        row = lax.broadcasted_iota(jnp.int32, (lc, lc), 0)
        col = lax.broadcasted_iota(jnp.int32, (lc, lc), 1)
        rel = (row - col).astype(F32)
        pos = lax.broadcasted_iota(jnp.int32, (lc, RET_DK), 0).astype(F32)
        for h in range(RET_HEADS):
            lg = _ret_log_gamma(h)
            dmat_ref[h] = jnp.where(rel >= 0.0, jnp.exp(rel * lg), 0.0)
            qdec_ref[h] = jnp.exp((pos + 1.0) * lg)
            kdec_ref[h] = jnp.exp((lc - 1.0 - pos) * lg)

    @pl.when(c == 0)
    def _():
        state_ref[...] = jnp.zeros_like(state_ref)

    ang = pos_ref[...] * invf_ref[...]
    cs = jnp.cos(ang)
    sn = jnp.sin(ang)
    cc = jnp.concatenate([cs, cs], axis=-1)
    ss = jnp.concatenate([-sn, sn], axis=-1)
    half = RET_DK // 2
    for h in range(RET_HEADS):
        sl = slice(h * RET_DK, (h + 1) * RET_DK)
        qh = q_ref[:, sl].astype(F32)
        kh = k_ref[:, sl].astype(F32)
        qr = qh * cc + pltpu.roll(qh, half, 1) * ss
        kr = (kh * cc + pltpu.roll(kh, half, 1) * ss) * (RET_DK ** -0.5)
        vb = v_ref[:, sl]
        scores = lax.dot_general(qr.astype(BF16), kr.astype(BF16), (((1,), (1,)), ((), ())),
                                 preferred_element_type=F32) * dmat_ref[h]
        inner = jnp.dot(scores.astype(BF16), vb, preferred_element_type=F32)
        st = state_ref[h]
        cross = jnp.dot((qr * qdec_ref[h]).astype(BF16), st.astype(BF16), preferred_element_type=F32)
        kdt = (kr * kdec_ref[h]).T.astype(BF16)
        state_ref[h] = st * math.exp(lc * _ret_log_gamma(h)) + jnp.dot(kdt, vb, preferred_element_type=F32)
        o = inner + cross
        o = o * lax.rsqrt(jnp.mean(o * o, axis=-1, keepdims=True) + EPS)
        gh = g_ref[:, sl].astype(F32)
        o_ref[:, sl] = o * (gh * _sigmoid(gh))


def _retention(z3, posf, inv_freq):
    bsz, seq, _ = z3.shape
    lc = 256
    blk = lambda k: pl.BlockSpec((None, lc, RET_QK_WIDTH), lambda b, c, k=k: (b, c, k))
    return pl.pallas_call(
        functools.partial(_retention_kernel, lc=lc),
        grid=(bsz, seq // lc),
        in_specs=[blk(COL_Q // 512), blk(COL_K // 512), blk(COL_V // 512), blk(COL_G // 512),
                  pl.BlockSpec((None, lc, 1), lambda b, c: (b, c, 0)),
                  pl.BlockSpec((1, RET_DK // 2), lambda b, c: (0, 0))],
        out_specs=pl.BlockSpec((None, lc, RET_V_WIDTH), lambda b, c: (b, c, 0)),
        out_shape=jax.ShapeDtypeStruct((bsz, seq, RET_V_WIDTH), F32),
        scratch_shapes=[pltpu.VMEM((RET_HEADS, RET_DK, RET_DV), F32),
                        pltpu.VMEM((RET_HEADS, lc, lc), F32),
                        pltpu.VMEM((RET_HEADS, lc, RET_DK), F32),
                        pltpu.VMEM((RET_HEADS, lc, RET_DK), F32)],
        compiler_params=_cparams(2),
    )(z3, z3, z3, z3, posf, inv_freq)


S5_SCAN_COLS = 512


def _s5_kernel(u_ref, bmat_ref, lam_ref, cmat_ref, d_ref, wglu_ref, o_ref, x_ref, state_ref, *, ts, bsz):
    @pl.when(pl.program_id(0) == 0)
    def _():
        state_ref[...] = jnp.zeros_like(state_ref)

    u = u_ref[...].reshape(ts * bsz, S5_WIDTH)
    x_ref[...] = jnp.dot(u.astype(BF16), bmat_ref[...], preferred_element_type=F32)

    for c0 in range(0, S5_NSTATE, S5_SCAN_COLS):
        re = pl.ds(c0, S5_SCAN_COLS)
        im = pl.ds(S5_NSTATE + c0, S5_SCAN_COLS)
        lr = jnp.broadcast_to(lam_ref[0:1, c0:c0 + S5_SCAN_COLS], (bsz, S5_SCAN_COLS))
        li = jnp.broadcast_to(lam_ref[1:2, c0:c0 + S5_SCAN_COLS], (bsz, S5_SCAN_COLS))

        def step(t, carry):
            xr, xi = carry
            rows = pl.ds(pl.multiple_of(t * bsz, bsz), bsz)
            nr = lr * xr - li * xi + x_ref[rows, re]
            ni = lr * xi + li * xr + x_ref[rows, im]
            x_ref[rows, re] = nr
            x_ref[rows, im] = ni
            return nr, ni

        xr, xi = lax.fori_loop(0, ts, step, (state_ref[:, re], state_ref[:, im]), unroll=8)
        state_ref[:, re] = xr
        state_ref[:, im] = xi

    y = jnp.dot(x_ref[...].astype(BF16), cmat_ref[...], preferred_element_type=F32) + d_ref[...] * u
    y = _gelu(y)
    lg = jnp.dot(y.astype(BF16), wglu_ref[...], preferred_element_type=F32)
    out = lg[:, :S5_WIDTH] * _sigmoid(lg[:, S5_WIDTH:])
    o_ref[...] = out.reshape(ts, bsz, S5_WIDTH)


def _s5(u_sb, bmat, lam, cmat, d, wglu_bf):
    seq, bsz, _ = u_sb.shape
    ts = 256
    full = lambda a: pl.BlockSpec(a.shape, lambda i: (0,) * a.ndim)
    return pl.pallas_call(
        functools.partial(_s5_kernel, ts=ts, bsz=bsz),
        grid=(seq // ts,),
        in_specs=[pl.BlockSpec((ts, bsz, S5_WIDTH), lambda i: (i, 0, 0)),
                  full(bmat), full(lam), full(cmat), full(d), full(wglu_bf)],
        out_specs=pl.BlockSpec((ts, bsz, S5_WIDTH), lambda i: (i, 0, 0)),
        out_shape=jax.ShapeDtypeStruct((seq, bsz, S5_WIDTH), F32),
        scratch_shapes=[pltpu.VMEM((ts * bsz, 2 * S5_NSTATE), F32),
                        pltpu.VMEM((bsz, 2 * S5_NSTATE), F32)],
        compiler_params=_cparams(1),
    )(u_sb, bmat, lam, cmat, d, wglu_bf)


def _memkv_kernel(m_ref, g_ref, w_ref, o_ref):
    mn = _rms(m_ref[...], g_ref[...]).astype(BF16)
    o_ref[...] = jnp.dot(mn, w_ref[...], preferred_element_type=F32)


def _memkv(mem, gain, w_bf):
    bsz, m, d = mem.shape
    n = w_bf.shape[1]
    return pl.pallas_call(
        _memkv_kernel,
        grid=(bsz,),
        in_specs=[pl.BlockSpec((None, m, d), lambda b: (b, 0, 0)),
                  pl.BlockSpec((1, d), lambda b: (0, 0)),
                  pl.BlockSpec((d, n), lambda b: (0, 0))],
        out_specs=pl.BlockSpec((None, m, n), lambda b: (b, 0, 0)),
        out_shape=jax.ShapeDtypeStruct((bsz, m, n), F32),
        compiler_params=_cparams(1),
    )(mem, gain, w_bf)


def _merge_kernel(x_ref, ret_ref, ssm_ref, qm_ref, g0, g1, g2, g3, g4, g5, kv_ref,
                  wr_ref, ws_ref, wm_ref, wo_ref, o_ref):
    kv = kv_ref[...]
    q = qm_ref[...]
    heads = []
    for h in range(MEM_HEADS):
        sl = slice(h * MEM_HEAD_DIM, (h + 1) * MEM_HEAD_DIM)
        kh = kv[:, sl].astype(BF16)
        vh = kv[:, MEM_WIDTH + h * MEM_HEAD_DIM:MEM_WIDTH + (h + 1) * MEM_HEAD_DIM].astype(BF16)
        s = lax.dot_general(q[:, sl].astype(BF16), kh, (((1,), (1,)), ((), ())),
                            preferred_element_type=F32) * (MEM_HEAD_DIM ** -0.5)
        s = s - jnp.max(s, axis=-1, keepdims=True)
        p = jnp.exp(s)
        p = p / jnp.sum(p, axis=-1, keepdims=True)
        heads.append(jnp.dot(p.astype(BF16), vh, preferred_element_type=F32))
    mattn = jnp.concatenate(heads, axis=-1)

    pr = jnp.dot(ret_ref[...].astype(BF16), wr_ref[...], preferred_element_type=F32)
    ps = jnp.dot(ssm_ref[...].astype(BF16), ws_ref[...], preferred_element_type=F32)
    pm = jnp.dot(mattn.astype(BF16), wm_ref[...], preferred_element_type=F32)
    gr = jnp.concatenate([g0[...], g1[...]], axis=-1).astype(F32)
    gs = jnp.concatenate([g2[...], g3[...]], axis=-1).astype(F32)
    gm = jnp.concatenate([g4[...], g5[...]], axis=-1).astype(F32)
    merged = _sigmoid(gr) * pr + _sigmoid(gs) * ps + _sigmoid(gm) * pm
    o_ref[...] = x_ref[...] + jnp.dot(merged.astype(BF16), wo_ref[...], preferred_element_type=F32)


def _merge(x, ret, ssm, z3, kv, wr, ws, wm, wo):
    bsz, seq, d = x.shape
    tm = 1024
    gw = 512
    gate = lambda k: pl.BlockSpec((None, tm, gw), lambda b, i, k=k: (b, i, COL_GATES // gw + k))
    full = lambda a: pl.BlockSpec(a.shape, lambda b, i: (0,) * a.ndim)
    return pl.pallas_call(
        _merge_kernel,
        grid=(bsz, seq // tm),
        in_specs=[pl.BlockSpec((None, tm, d), lambda b, i: (b, i, 0)),
                  pl.BlockSpec((None, tm, RET_V_WIDTH), lambda b, i: (b, i, 0)),
                  pl.BlockSpec((None, tm, S5_WIDTH), lambda b, i: (b, i, 0)),
                  pl.BlockSpec((None, tm, MEM_WIDTH), lambda b, i: (b, i, COL_QM // MEM_WIDTH)),
                  gate(0), gate(1), gate(2), gate(3), gate(4), gate(5),
                  pl.BlockSpec((None,) + kv.shape[1:], lambda b, i: (b, 0, 0)),
                  full(wr), full(ws), full(wm), full(wo)],
        out_specs=pl.BlockSpec((None, tm, d), lambda b, i: (b, i, 0)),
        out_shape=jax.ShapeDtypeStruct((bsz, seq, d), F32),
        compiler_params=_cparams(2),
    )(x, ret, ssm, z3, z3, z3, z3, z3, z3, z3, kv, wr, ws, wm, wo)


NEG_INF = float("-inf")
PEER_CAND = [(i, j) for i in range(PEER_TOPK) for j in range(PEER_TOPK) if (i + 1) * (j + 1) <= PEER_TOPK]


def _bitonic_merge_desc(v):
    v = list(v)
    j = len(v) // 2
    while j >= 1:
        for i in range(len(v)):
            l = i ^ j
            if l > i:
                v[i], v[l] = jnp.maximum(v[i], v[l]), jnp.minimum(v[i], v[l])
        j //= 2
    return v


def _sort_desc(v):
    v = list(v)
    n = len(v)
    k = 2
    while k <= n:
        j = k // 2
        while j >= 1:
            for i in range(n):
                l = i ^ j
                if l > i:
                    hi, lo = jnp.maximum(v[i], v[l]), jnp.minimum(v[i], v[l])
                    v[i], v[l] = (hi, lo) if (i & k) == 0 else (lo, hi)
            j //= 2
        k *= 2
    return v


def _peer_route_chunk(tc, s1t_ref, s2_ref, c1t_ref, n1t_ref, r2_ref, e2_ref, a_ref, b_ref):
    k = PEER_TOPK
    lanes = pl.ds(pl.multiple_of(tc * LANES, LANES), LANES)
    top = _sort_desc([s1t_ref[i, :, lanes] for i in range(k)])
    for g in range(1, PEER_NKEYS // k):
        grp = _sort_desc([s1t_ref[g * k + i, :, lanes] for i in range(k)])
        top = _bitonic_merge_desc([jnp.maximum(top[i], grp[k - 1 - i]) for i in range(k)])
    for r in range(k):
        a_ref[r, :, lanes] = top[r]
    for h in range(PEER_HEADS):
        s = s2_ref[h, :, lanes]
        rank = jnp.full(s.shape, float(k), F32)
        for r in range(k):
            m = jnp.max(s, axis=0, keepdims=True)
            b_ref[r, pl.ds(h, 1), lanes] = m
            hit = s == m
            rank = jnp.where(hit, float(r), rank)
            s = jnp.where(hit, NEG_INF, s)
        r2_ref[h, tc] = rank.astype(BF16)
    a = [a_ref[r, :, lanes] for r in range(k)]
    b = [b_ref[r, :, lanes] for r in range(k)]
    ea = [jnp.exp(v - a[0]) for v in a]
    eb = [jnp.exp(v - b[0]) for v in b]
    cand = [a[i] + b[j] for i, j in PEER_CAND]
    work = cand
    for r in range(k):
        tau = functools.reduce(jnp.maximum, work)
        if r + 1 < k:
            work = [jnp.where(w == tau, NEG_INF, w) for w in work]
    zsum = functools.reduce(
        jnp.add, [jnp.where(c >= tau, ea[i] * eb[j], 0.0) for c, (i, j) in zip(cand, PEER_CAND)])
    cnt = [functools.reduce(jnp.add, [jnp.where(a[i] + b[j] >= tau, 1.0, 0.0) for j in range(k)]) for i in range(k)]
    inv_z = 0.5 / zsum

    def group(g, carry):
        rows = pl.ds(pl.multiple_of(g * SUBLANES, SUBLANES), SUBLANES)
        s1 = s1t_ref[rows, :, lanes]
        n1 = jnp.zeros(s1.shape, F32)
        for i in range(k):
            n1 = jnp.where(s1 == a[i][None], cnt[i][None], n1)
        n1t_ref[rows, :, lanes] = n1
        c1t_ref[rows, :, lanes] = jnp.exp(s1 - a[0][None]) * inv_z[None]
        return carry

    lax.fori_loop(0, PEER_NKEYS // SUBLANES, group, 0)
    for h in range(PEER_HEADS):
        e2_ref[h, tc] = jnp.exp(s2_ref[h, :, lanes] - b[0][h:h + 1, :]).astype(BF16)


BF16_ROWS = 16
PEER_MM1_ROWS = 512


def _gelu_consts_bf16(shape):
    c = math.sqrt(2.0 / math.pi)
    return jnp.full(shape, c, F32).astype(BF16), jnp.full(shape, c * 0.044715, F32).astype(BF16)


def _twice_gelu_bf16(x, c0, c1):
    u = x * (c0 + c1 * (x * x))
    return x + x * jnp.tanh(u)


def _peer_elementwise(hid_ref, at_ref, n1s_ref, c1s_ref, r2_ref, e2_ref, *, tb, eb):
    n_slab = PEER_NKEYS // BF16_ROWS
    c0, c1 = _gelu_consts_bf16((BF16_ROWS, LANES))
    for r in range(eb // PEER_NKEYS):
        for tc in range(tb // LANES):
            lanes = pl.ds(tc * LANES, LANES)
            w = [None] * n_slab
            for h in range(PEER_HEADS):
                nb = jnp.broadcast_to(n1s_ref[r, pl.ds(h, 1), lanes], (BF16_ROWS, LANES)).astype(BF16)
                cb = jnp.broadcast_to(c1s_ref[r, pl.ds(h, 1), lanes], (BF16_ROWS, LANES)).astype(BF16)
                for k in range(n_slab):
                    rows = pl.ds(k * BF16_ROWS, BF16_ROWS)
                    contrib = jnp.where(r2_ref[h, tc, rows, :] < nb, e2_ref[h, tc, rows, :] * cb, 0)
                    w[k] = contrib if w[k] is None else w[k] + contrib
            for k in range(n_slab):
                rows = pl.ds(r * PEER_NKEYS + k * BF16_ROWS, BF16_ROWS)
                at_ref[tc, rows, :] = _twice_gelu_bf16(hid_ref[tc, rows, :], c0, c1) * w[k]


def _peer_kernel(h1_ref, gf_ref, wq1t_ref, wq2t_ref, kperm_ref, keys2_ref, u_ref, vt_ref, gfin_ref, y_ref,
                 xnt_ref, s1t_ref, c1t_ref, n1t_ref, s2_ref, r2_ref, e2_ref, a_ref, b_ref,
                 hid_ref, at_ref, acc_ref, n1s_ref, c1s_ref, *, tb, eb):
    s = pl.program_id(1)
    rows_per_cell = eb // PEER_NKEYS
    n_tc = tb // LANES

    @pl.when(s == 0)
    def _():
        xn = _rms(h1_ref[...], gf_ref[...])
        xnt_ref[...] = xn.T.astype(BF16)
        q1t = jnp.dot(wq1t_ref[...], xnt_ref[...], preferred_element_type=F32).astype(BF16)
        q2t = jnp.dot(wq2t_ref[...], xnt_ref[...], preferred_element_type=F32).astype(BF16)
        s1t = jnp.dot(kperm_ref[...], q1t, preferred_element_type=F32)
        s1t_ref[...] = s1t.reshape(PEER_NKEYS, PEER_HEADS, tb)
        for h in range(PEER_HEADS):
            s2_ref[h] = jnp.dot(keys2_ref[h], q2t[h * PEER_HALF:(h + 1) * PEER_HALF],
                                preferred_element_type=F32)
        def route(tc, carry):
            _peer_route_chunk(tc, s1t_ref, s2_ref, c1t_ref, n1t_ref, r2_ref, e2_ref, a_ref, b_ref)
            return carry

        lax.fori_loop(0, tb // LANES, route, 0)
        acc_ref[...] = jnp.zeros_like(acc_ref)

    first = pl.multiple_of(s * rows_per_cell, rows_per_cell)
    n1s_ref[...] = n1t_ref[pl.ds(first, rows_per_cell)]
    c1s_ref[...] = c1t_ref[pl.ds(first, rows_per_cell)]
    for m0 in range(0, eb, PEER_MM1_ROWS):
        hid = jnp.dot(u_ref[m0:m0 + PEER_MM1_ROWS, :], xnt_ref[...], preferred_element_type=F32).astype(BF16)
        for tc in range(n_tc):
            hid_ref[tc, m0:m0 + PEER_MM1_ROWS, :] = hid[:, tc * LANES:(tc + 1) * LANES]
    _peer_elementwise(hid_ref, at_ref, n1s_ref, c1s_ref, r2_ref, e2_ref, tb=tb, eb=eb)
    at_full = jnp.concatenate([at_ref[tc] for tc in range(n_tc)], axis=1)
    acc_ref[...] += jnp.dot(vt_ref[...], at_full, preferred_element_type=F32)

    @pl.when(s == pl.num_programs(1) - 1)
    def _():
        h2 = h1_ref[...] + acc_ref[...].T
        y_ref[...] = _rms(h2, gfin_ref[...])


def _peer(h1, g_ffn, wq1t_bf, wq2t_bf, kperm_bf, keys2_bf, u_bf, vt_bf, g_final):
    t, d = h1.shape
    n_exp = u_bf.shape[0]
    tb, eb = 512, 2048
    full = lambda a: pl.BlockSpec(a.shape, lambda i, s: (0,) * a.ndim)
    route_t = pltpu.VMEM((PEER_NKEYS, PEER_HEADS, tb), F32)
    ranks = pltpu.VMEM((PEER_TOPK, PEER_HEADS, tb), F32)

    def chunked(*lead):
        return pltpu.VMEM(lead[:-1] + (tb // LANES, lead[-1], LANES), BF16)

    return pl.pallas_call(
        functools.partial(_peer_kernel, tb=tb, eb=eb),
        grid=(t // tb, n_exp // eb),
        in_specs=[pl.BlockSpec((tb, d), lambda i, s: (i, 0), pipeline_mode=pl.Buffered(1)),
                  full(g_ffn), full(wq1t_bf), full(wq2t_bf), full(kperm_bf), full(keys2_bf),
                  pl.BlockSpec((eb, d), lambda i, s: (s, 0)),
                  pl.BlockSpec((d, eb), lambda i, s: (0, s)),
                  full(g_final)],
        out_specs=pl.BlockSpec((tb, d), lambda i, s: (i, 0), pipeline_mode=pl.Buffered(1)),
        out_shape=jax.ShapeDtypeStruct((t, d), F32),
        scratch_shapes=[pltpu.VMEM((d, tb), BF16),
                        route_t,
                        route_t,
                        route_t,
                        pltpu.VMEM((PEER_HEADS, PEER_NKEYS, tb), F32),
                        chunked(PEER_HEADS, PEER_NKEYS),
                        chunked(PEER_HEADS, PEER_NKEYS),
                        ranks, ranks,
                        chunked(eb),
                        chunked(eb),
                        pltpu.VMEM((d, tb), F32),
                        pltpu.VMEM((eb // PEER_NKEYS, PEER_HEADS, tb), F32),
                        pltpu.VMEM((eb // PEER_NKEYS, PEER_HEADS, tb), F32)],
        compiler_params=pltpu.CompilerParams(dimension_semantics=("arbitrary", "arbitrary"),
                                             vmem_limit_bytes=PEER_VMEM_LIMIT),
    )(h1, g_ffn, wq1t_bf, wq2t_bf, kperm_bf, keys2_bf, u_bf, vt_bf, g_final)


def _block_diag(blocks):
    g, r, c = blocks.shape
    eye = jnp.eye(g, dtype=blocks.dtype)
    return jnp.einsum("grc,gk->grkc", blocks, eye).reshape(g * r, g * c)


def _peer_params(w_q, keys):
    d = w_q.shape[0]
    wq = w_q.reshape(d, PEER_HEADS, 2, PEER_HALF)
    wq1t = wq[:, :, 0, :].reshape(d, PEER_HEADS * PEER_HALF).T
    wq2t = wq[:, :, 1, :].reshape(d, PEER_HEADS * PEER_HALF).T
    eye = jnp.eye(PEER_HEADS, dtype=keys.dtype)
    kperm = jnp.einsum("hid,hk->ihkd", keys[:, 0], eye).reshape(PEER_NKEYS * PEER_HEADS, PEER_HEADS * PEER_HALF)
    return wq1t.astype(BF16), wq2t.astype(BF16), kperm.astype(BF16), keys[:, 1].astype(BF16)


def kernel(x, mem, positions, norm_mix, w_in, norm_mem, w_mem_kv, s5_a_re, s5_a_im, s5_log_dt, s5_b_re, s5_b_im, s5_c_re, s5_c_im, s5_d, s5_w_glu, w_branch_ret, w_branch_s5, w_branch_mem, w_out, norm_ffn, peer_w_q, peer_keys, peer_u, peer_v, norm_final):
    bsz, seq, d = x.shape
    depth = w_in.shape[0]
    assert depth == 1, "the final RMSNorm is fused into the PEER call of a single layer"
    assert bsz % SUBLANES == 0 and seq % 512 == 0 and d == 1024
    half = RET_DK // 2
    inv_freq = (ROPE_BASE ** (-jnp.arange(half, dtype=F32) / half)).reshape(1, half)
    posf = positions.astype(F32).reshape(bsz, seq, 1)
    row = lambda v: v.reshape(1, -1).astype(F32)

    h = x
    for l in range(depth):
        lam_re, lam_im, bb_re, bb_im = _s5_discretize(s5_a_re[l], s5_a_im[l], s5_log_dt[l], s5_b_re[l], s5_b_im[l])
        bmat = jnp.concatenate([_block_diag(bb_re), _block_diag(bb_im)], axis=1).astype(BF16)
        lam = jnp.stack([lam_re.reshape(-1), lam_im.reshape(-1)])
        c_re_t = jnp.transpose(s5_c_re[l], (0, 2, 1))
        c_im_t = jnp.transpose(s5_c_im[l], (0, 2, 1))
        cmat = jnp.concatenate([_block_diag(c_re_t), -_block_diag(c_im_t)], axis=0).astype(BF16)

        z = _inproj(h.reshape(bsz * seq, d), row(norm_mix[l]), w_in[l].astype(BF16))
        z3 = z.reshape(bsz, seq, IN_COLS)
        ret = _retention(z3, posf, inv_freq)
        u_sb = jnp.transpose(z3[:, :, COL_U:COL_U + S5_WIDTH], (1, 0, 2)).astype(F32)
        ssm_sb = _s5(u_sb, bmat, lam, cmat, row(s5_d[l]), s5_w_glu[l].astype(BF16))
        ssm = jnp.transpose(ssm_sb, (1, 0, 2))
        kv = _memkv(mem, row(norm_mem[l]), w_mem_kv[l].astype(BF16))
        h1 = _merge(h, ret, ssm, z3, kv, w_branch_ret[l].astype(BF16), w_branch_s5[l].astype(BF16),
                    w_branch_mem[l].astype(BF16), w_out[l].astype(BF16))
        wq1t, wq2t, kperm, keys2 = _peer_params(peer_w_q[l], peer_keys[l])
        y = _peer(h1.reshape(bsz * seq, d), row(norm_ffn[l]), wq1t, wq2t, kperm, keys2,
                  peer_u[l].astype(BF16), peer_v[l].T.astype(BF16), row(norm_final))
        h = y.reshape(bsz, seq, d)
    return h
```

```python
import functools
import math

import jax
import jax.numpy as jnp
from jax import lax
from jax.experimental import pallas as pl
from jax.experimental.pallas import tpu as pltpu

F32 = jnp.float32
BF16 = jnp.bfloat16

EPS = 1e-6
RET_HEADS = 4
RET_DK = 128
RET_DV = 128
RET_QK_WIDTH = RET_HEADS * RET_DK
RET_V_WIDTH = RET_HEADS * RET_DV
ROPE_BASE = 10000.0
S5_GROUPS = 16
S5_GROUP_DIM = 16
S5_STATE = 64
S5_WIDTH = S5_GROUPS * S5_GROUP_DIM
S5_NSTATE = S5_GROUPS * S5_STATE
MEM_HEADS = 4
MEM_HEAD_DIM = 64
MEM_WIDTH = MEM_HEADS * MEM_HEAD_DIM
PEER_HEADS = 8
PEER_NKEYS = 128
PEER_HALF = 128
PEER_TOPK = 16

LANES = 128
SUBLANES = 8
VMEM_LIMIT = 48 * 1024 * 1024
PEER_VMEM_LIMIT = 60 * 1024 * 1024

COL_Q, COL_K, COL_V, COL_G = 0, 512, 1024, 1536
COL_U, COL_QM, COL_GATES = 2048, 2304, 2560
IN_COLS = 5632


def _cparams(n_axes):
    return pltpu.CompilerParams(dimension_semantics=("arbitrary",) * n_axes, vmem_limit_bytes=VMEM_LIMIT)


def _rms(x, gain):
    return x * lax.rsqrt(jnp.mean(x * x, axis=-1, keepdims=True) + EPS) * gain


def _gelu(x):
    c = math.sqrt(2.0 / math.pi)
    return 0.5 * x * (1.0 + jnp.tanh(c * (x + 0.044715 * (x * x * x))))


def _sigmoid(x):
    return 1.0 / (1.0 + jnp.exp(-x))


def _s5_disc_kernel(are_ref, aim_ref, ldt_ref, bre_ref, bim_ref, lre_ref, lim_ref, bbre_ref, bbim_ref):
    a_re = are_ref[...]
    a_im = aim_ref[...]
    dt = jnp.exp(ldt_ref[...])
    mag = jnp.exp(a_re * dt)
    lam_re = mag * jnp.cos(a_im * dt)
    lam_im = mag * jnp.sin(a_im * dt)
    den = a_re * a_re + a_im * a_im
    nr = lam_re - 1.0
    ni = lam_im
    cr = (nr * a_re + ni * a_im) / den
    ci = (ni * a_re - nr * a_im) / den
    b_re = bre_ref[...]
    b_im = bim_ref[...]
    lre_ref[...] = lam_re
    lim_ref[...] = lam_im
    bbre_ref[...] = cr * b_re - ci * b_im
    bbim_ref[...] = cr * b_im + ci * b_re


def _s5_discretize(a_re, a_im, log_dt, b_re, b_im):
    g, p = a_re.shape
    h = b_re.shape[-1]
    rep = lambda a: jnp.repeat(a, h, axis=0)
    b_re_t = jnp.transpose(b_re, (0, 2, 1)).reshape(g * h, p)
    b_im_t = jnp.transpose(b_im, (0, 2, 1)).reshape(g * h, p)
    shp = jax.ShapeDtypeStruct((g * h, p), F32)
    lam_re, lam_im, bb_re, bb_im = pl.pallas_call(
        _s5_disc_kernel, out_shape=(shp, shp, shp, shp),
    )(rep(a_re), rep(a_im), rep(log_dt[:, None]), b_re_t, b_im_t)
    return lam_re[::h], lam_im[::h], bb_re.reshape(g, h, p), bb_im.reshape(g, h, p)


def _inproj_kernel(x_ref, g_ref, w_ref, z_ref, xn_ref):
    @pl.when(pl.program_id(1) == 0)
    def _():
        xn_ref[...] = _rms(x_ref[...], g_ref[...]).astype(BF16)

    z_ref[...] = jnp.dot(xn_ref[...], w_ref[...], preferred_element_type=F32).astype(BF16)


def _inproj(x2, gain, w_bf):
    t, d = x2.shape
    tm, tn = 512, IN_COLS
    return pl.pallas_call(
        _inproj_kernel,
        grid=(t // tm, IN_COLS // tn),
        in_specs=[pl.BlockSpec((tm, d), lambda i, j: (i, 0)),
                  pl.BlockSpec((1, d), lambda i, j: (0, 0)),
                  pl.BlockSpec((d, tn), lambda i, j: (0, j), pipeline_mode=pl.Buffered(1))],
        out_specs=pl.BlockSpec((tm, tn), lambda i, j: (i, j)),
        out_shape=jax.ShapeDtypeStruct((t, IN_COLS), BF16),
        scratch_shapes=[pltpu.VMEM((tm, d), BF16)],
        compiler_params=_cparams(2),
    )(x2, gain, w_bf)


def _ret_log_gamma(h):
    return math.log1p(-(2.0 ** (-5.0 - h)))


def _retention_kernel(q_ref, k_ref, v_ref, g_ref, pos_ref, invf_ref, o_ref,
                      state_ref, dmat_ref, qdec_ref, kdec_ref, *, lc):
    b = pl.program_id(0)
    c = pl.program_id(1)

    @pl.when((b == 0) & (c == 0))
    def _():
        row = lax.broadcasted_iota(jnp.int32, (lc, lc), 0)
        col = lax.broadcasted_iota(jnp.int32, (lc, lc), 1)
        rel = (row - col).astype(F32)
        pos = lax.broadcasted_iota(jnp.int32, (lc, RET_DK), 0).astype(F32)
        for h in range(RET_HEADS):
            lg = _ret_log_gamma(h)
            dmat_ref[h] = jnp.where(rel >= 0.0, jnp.exp(rel * lg), 0.0)
            qdec_ref[h] = jnp.exp((pos + 1.0) * lg)
            kdec_ref[h] = jnp.exp((lc - 1.0 - pos) * lg)

    @pl.when(c == 0)
    def _():
        state_ref[...] = jnp.zeros_like(state_ref)

    ang = pos_ref[...] * invf_ref[...]
    cs = jnp.cos(ang)
    sn = jnp.sin(ang)
    cc = jnp.concatenate([cs, cs], axis=-1)
    ss = jnp.concatenate([-sn, sn], axis=-1)
    half = RET_DK // 2
    for h in range(RET_HEADS):
        sl = slice(h * RET_DK, (h + 1) * RET_DK)
        qh = q_ref[:, sl].astype(F32)
        kh = k_ref[:, sl].astype(F32)
        qr = qh * cc + pltpu.roll(qh, half, 1) * ss
        kr = (kh * cc + pltpu.roll(kh, half, 1) * ss) * (RET_DK ** -0.5)
        vb = v_ref[:, sl]
        scores = lax.dot_general(qr.astype(BF16), kr.astype(BF16), (((1,), (1,)), ((), ())),
                                 preferred_element_type=F32) * dmat_ref[h]
        inner = jnp.dot(scores.astype(BF16), vb, preferred_element_type=F32)
        st = state_ref[h]
        cross = jnp.dot((qr * qdec_ref[h]).astype(BF16), st.astype(BF16), preferred_element_type=F32)
        kdt = (kr * kdec_ref[h]).T.astype(BF16)
        state_ref[h] = st * math.exp(lc * _ret_log_gamma(h)) + jnp.dot(kdt, vb, preferred_element_type=F32)
        o = inner + cross
        o = o * lax.rsqrt(jnp.mean(o * o, axis=-1, keepdims=True) + EPS)
        gh = g_ref[:, sl].astype(F32)
        o_ref[:, sl] = o * (gh * _sigmoid(gh))


def _retention(z3, posf, inv_freq):
    bsz, seq, _ = z3.shape
    lc = 256
    blk = lambda k: pl.BlockSpec((None, lc, RET_QK_WIDTH), lambda b, c, k=k: (b, c, k))
    return pl.pallas_call(
        functools.partial(_retention_kernel, lc=lc),
        grid=(bsz, seq // lc),
        in_specs=[blk(COL_Q // 512), blk(COL_K // 512), blk(COL_V // 512), blk(COL_G // 512),
                  pl.BlockSpec((None, lc, 1), lambda b, c: (b, c, 0)),
                  pl.BlockSpec((1, RET_DK // 2), lambda b, c: (0, 0))],
        out_specs=pl.BlockSpec((None, lc, RET_V_WIDTH), lambda b, c: (b, c, 0)),
        out_shape=jax.ShapeDtypeStruct((bsz, seq, RET_V_WIDTH), F32),
        scratch_shapes=[pltpu.VMEM((RET_HEADS, RET_DK, RET_DV), F32),
                        pltpu.VMEM((RET_HEADS, lc, lc), F32),
                        pltpu.VMEM((RET_HEADS, lc, RET_DK), F32),
                        pltpu.VMEM((RET_HEADS, lc, RET_DK), F32)],
        compiler_params=_cparams(2),
    )(z3, z3, z3, z3, posf, inv_freq)


S5_SCAN_COLS = 512


def _s5_kernel(u_ref, bmat_ref, lam_ref, cmat_ref, d_ref, wglu_ref, o_ref, x_ref, state_ref, *, ts, bsz):
    @pl.when(pl.program_id(0) == 0)
    def _():
        state_ref[...] = jnp.zeros_like(state_ref)

    u = u_ref[...].reshape(ts * bsz, S5_WIDTH)
    x_ref[...] = jnp.dot(u.astype(BF16), bmat_ref[...], preferred_element_type=F32)

    for c0 in range(0, S5_NSTATE, S5_SCAN_COLS):
        re = pl.ds(c0, S5_SCAN_COLS)
        im = pl.ds(S5_NSTATE + c0, S5_SCAN_COLS)
        lr = jnp.broadcast_to(lam_ref[0:1, c0:c0 + S5_SCAN_COLS], (bsz, S5_SCAN_COLS))
        li = jnp.broadcast_to(lam_ref[1:2, c0:c0 + S5_SCAN_COLS], (bsz, S5_SCAN_COLS))

        def step(t, carry):
            xr, xi = carry
            rows = pl.ds(pl.multiple_of(t * bsz, bsz), bsz)
            nr = lr * xr - li * xi + x_ref[rows, re]
            ni = lr * xi + li * xr + x_ref[rows, im]
            x_ref[rows, re] = nr
            x_ref[rows, im] = ni
            return nr, ni

        xr, xi = lax.fori_loop(0, ts, step, (state_ref[:, re], state_ref[:, im]), unroll=8)
        state_ref[:, re] = xr
        state_ref[:, im] = xi

    y = jnp.dot(x_ref[...].astype(BF16), cmat_ref[...], preferred_element_type=F32) + d_ref[...] * u
    y = _gelu(y)
    lg = jnp.dot(y.astype(BF16), wglu_ref[...], preferred_element_type=F32)
    out = lg[:, :S5_WIDTH] * _sigmoid(lg[:, S5_WIDTH:])
    o_ref[...] = out.reshape(ts, bsz, S5_WIDTH)


def _s5(u_sb, bmat, lam, cmat, d, wglu_bf):
    seq, bsz, _ = u_sb.shape
    ts = 256
    full = lambda a: pl.BlockSpec(a.shape, lambda i: (0,) * a.ndim)
    return pl.pallas_call(
        functools.partial(_s5_kernel, ts=ts, bsz=bsz),
        grid=(seq // ts,),
        in_specs=[pl.BlockSpec((ts, bsz, S5_WIDTH), lambda i: (i, 0, 0)),
                  full(bmat), full(lam), full(cmat), full(d), full(wglu_bf)],
        out_specs=pl.BlockSpec((ts, bsz, S5_WIDTH), lambda i: (i, 0, 0)),
        out_shape=jax.ShapeDtypeStruct((seq, bsz, S5_WIDTH), F32),
        scratch_shapes=[pltpu.VMEM((ts * bsz, 2 * S5_NSTATE), F32),
                        pltpu.VMEM((bsz, 2 * S5_NSTATE), F32)],
        compiler_params=_cparams(1),
    )(u_sb, bmat, lam, cmat, d, wglu_bf)


def _memkv_kernel(m_ref, g_ref, w_ref, o_ref):
    mn = _rms(m_ref[...], g_ref[...]).astype(BF16)
    o_ref[...] = jnp.dot(mn, w_ref[...], preferred_element_type=F32)


def _memkv(mem, gain, w_bf):
    bsz, m, d = mem.shape
    n = w_bf.shape[1]
    return pl.pallas_call(
        _memkv_kernel,
        grid=(bsz,),
        in_specs=[pl.BlockSpec((None, m, d), lambda b: (b, 0, 0)),
                  pl.BlockSpec((1, d), lambda b: (0, 0)),
                  pl.BlockSpec((d, n), lambda b: (0, 0))],
        out_specs=pl.BlockSpec((None, m, n), lambda b: (b, 0, 0)),
        out_shape=jax.ShapeDtypeStruct((bsz, m, n), F32),
        compiler_params=_cparams(1),
    )(mem, gain, w_bf)


def _merge_kernel(x_ref, ret_ref, ssm_ref, qm_ref, g0, g1, g2, g3, g4, g5, kv_ref,
                  wr_ref, ws_ref, wm_ref, wo_ref, o_ref):
    kv = kv_ref[...]
    q = qm_ref[...]
    heads = []
    for h in range(MEM_HEADS):
        sl = slice(h * MEM_HEAD_DIM, (h + 1) * MEM_HEAD_DIM)
        kh = kv[:, sl].astype(BF16)
        vh = kv[:, MEM_WIDTH + h * MEM_HEAD_DIM:MEM_WIDTH + (h + 1) * MEM_HEAD_DIM].astype(BF16)
        s = lax.dot_general(q[:, sl].astype(BF16), kh, (((1,), (1,)), ((), ())),
                            preferred_element_type=F32) * (MEM_HEAD_DIM ** -0.5)
        s = s - jnp.max(s, axis=-1, keepdims=True)
        p = jnp.exp(s)
        p = p / jnp.sum(p, axis=-1, keepdims=True)
        heads.append(jnp.dot(p.astype(BF16), vh, preferred_element_type=F32))
    mattn = jnp.concatenate(heads, axis=-1)

    pr = jnp.dot(ret_ref[...].astype(BF16), wr_ref[...], preferred_element_type=F32)
    ps = jnp.dot(ssm_ref[...].astype(BF16), ws_ref[...], preferred_element_type=F32)
    pm = jnp.dot(mattn.astype(BF16), wm_ref[...], preferred_element_type=F32)
    gr = jnp.concatenate([g0[...], g1[...]], axis=-1).astype(F32)
    gs = jnp.concatenate([g2[...], g3[...]], axis=-1).astype(F32)
    gm = jnp.concatenate([g4[...], g5[...]], axis=-1).astype(F32)
    merged = _sigmoid(gr) * pr + _sigmoid(gs) * ps + _sigmoid(gm) * pm
    o_ref[...] = x_ref[...] + jnp.dot(merged.astype(BF16), wo_ref[...], preferred_element_type=F32)


def _merge(x, ret, ssm, z3, kv, wr, ws, wm, wo):
    bsz, seq, d = x.shape
    tm = 1024
    gw = 512
    gate = lambda k: pl.BlockSpec((None, tm, gw), lambda b, i, k=k: (b, i, COL_GATES // gw + k))
    full = lambda a: pl.BlockSpec(a.shape, lambda b, i: (0,) * a.ndim)
    return pl.pallas_call(
        _merge_kernel,
        grid=(bsz, seq // tm),
        in_specs=[pl.BlockSpec((None, tm, d), lambda b, i: (b, i, 0)),
                  pl.BlockSpec((None, tm, RET_V_WIDTH), lambda b, i: (b, i, 0)),
                  pl.BlockSpec((None, tm, S5_WIDTH), lambda b, i: (b, i, 0)),
                  pl.BlockSpec((None, tm, MEM_WIDTH), lambda b, i: (b, i, COL_QM // MEM_WIDTH)),
                  gate(0), gate(1), gate(2), gate(3), gate(4), gate(5),
                  pl.BlockSpec((None,) + kv.shape[1:], lambda b, i: (b, 0, 0)),
                  full(wr), full(ws), full(wm), full(wo)],
        out_specs=pl.BlockSpec((None, tm, d), lambda b, i: (b, i, 0)),
        out_shape=jax.ShapeDtypeStruct((bsz, seq, d), F32),
        compiler_params=_cparams(2),
    )(x, ret, ssm, z3, z3, z3, z3, z3, z3, z3, kv, wr, ws, wm, wo)


NEG_INF = float("-inf")
PEER_CAND = [(i, j) for i in range(PEER_TOPK) for j in range(PEER_TOPK) if (i + 1) * (j + 1) <= PEER_TOPK]


def _bitonic_merge_desc(v):
    v = list(v)
    j = len(v) // 2
    while j >= 1:
        for i in range(len(v)):
            l = i ^ j
            if l > i:
                v[i], v[l] = jnp.maximum(v[i], v[l]), jnp.minimum(v[i], v[l])
        j //= 2
    return v


def _sort_desc(v):
    v = list(v)
    n = len(v)
    k = 2
    while k <= n:
        j = k // 2
        while j >= 1:
            for i in range(n):
                l = i ^ j
                if l > i:
                    hi, lo = jnp.maximum(v[i], v[l]), jnp.minimum(v[i], v[l])
                    v[i], v[l] = (hi, lo) if (i & k) == 0 else (lo, hi)
            j //= 2
        k *= 2
    return v


def _peer_route_chunk(tc, s1t_ref, s2_ref, c1t_ref, n1t_ref, r2_ref, e2_ref, a_ref, b_ref):
    k = PEER_TOPK
    lanes = pl.ds(pl.multiple_of(tc * LANES, LANES), LANES)
    top = _sort_desc([s1t_ref[i, :, lanes] for i in range(k)])
    for g in range(1, PEER_NKEYS // k):
        grp = _sort_desc([s1t_ref[g * k + i, :, lanes] for i in range(k)])
        top = _bitonic_merge_desc([jnp.maximum(top[i], grp[k - 1 - i]) for i in range(k)])
    for r in range(k):
        a_ref[r, :, lanes] = top[r]
    for h in range(PEER_HEADS):
        s = s2_ref[h, :, lanes]
        rank = jnp.full(s.shape, float(k), F32)
        for r in range(k):
            m = jnp.max(s, axis=0, keepdims=True)
            b_ref[r, pl.ds(h, 1), lanes] = m
            hit = s == m
            rank = jnp.where(hit, float(r), rank)
            s = jnp.where(hit, NEG_INF, s)
        r2_ref[h, tc] = rank.astype(BF16)
    a = [a_ref[r, :, lanes] for r in range(k)]
    b = [b_ref[r, :, lanes] for r in range(k)]
    ea = [jnp.exp(v - a[0]) for v in a]
    eb = [jnp.exp(v - b[0]) for v in b]
    cand = [a[i] + b[j] for i, j in PEER_CAND]
    work = cand
    for r in range(k):
        tau = functools.reduce(jnp.maximum, work)
        if r + 1 < k:
            work = [jnp.where(w == tau, NEG_INF, w) for w in work]
    zsum = functools.reduce(
        jnp.add, [jnp.where(c >= tau, ea[i] * eb[j], 0.0) for c, (i, j) in zip(cand, PEER_CAND)])
    cnt = [functools.reduce(jnp.add, [jnp.where(a[i] + b[j] >= tau, 1.0, 0.0) for j in range(k)]) for i in range(k)]
    inv_z = 0.5 / zsum

    def group(g, carry):
        rows = pl.ds(pl.multiple_of(g * SUBLANES, SUBLANES), SUBLANES)
        s1 = s1t_ref[rows, :, lanes]
        n1 = jnp.zeros(s1.shape, F32)
        for i in range(k):
            n1 = jnp.where(s1 == a[i][None], cnt[i][None], n1)
        n1t_ref[rows, :, lanes] = n1
        c1t_ref[rows, :, lanes] = jnp.exp(s1 - a[0][None]) * inv_z[None]
        return carry

    lax.fori_loop(0, PEER_NKEYS // SUBLANES, group, 0)
    for h in range(PEER_HEADS):
        e2_ref[h, tc] = jnp.exp(s2_ref[h, :, lanes] - b[0][h:h + 1, :]).astype(BF16)


BF16_ROWS = 16
PEER_MM1_ROWS = 512


def _gelu_consts_bf16(shape):
    c = math.sqrt(2.0 / math.pi)
    return jnp.full(shape, c, F32).astype(BF16), jnp.full(shape, c * 0.044715, F32).astype(BF16)


def _twice_gelu_bf16(x, c0, c1):
    u = x * (c0 + c1 * (x * x))
    return x + x * jnp.tanh(u)


def _peer_elementwise(hid_ref, at_ref, n1s_ref, c1s_ref, r2_ref, e2_ref, *, tb, eb):
    n_slab = PEER_NKEYS // BF16_ROWS
    c0, c1 = _gelu_consts_bf16((BF16_ROWS, LANES))
    for r in range(eb // PEER_NKEYS):
        for tc in range(tb // LANES):
            lanes = pl.ds(tc * LANES, LANES)
            w = [None] * n_slab
            for h in range(PEER_HEADS):
                nb = jnp.broadcast_to(n1s_ref[r, pl.ds(h, 1), lanes], (BF16_ROWS, LANES)).astype(BF16)
                cb = jnp.broadcast_to(c1s_ref[r, pl.ds(h, 1), lanes], (BF16_ROWS, LANES)).astype(BF16)
                for k in range(n_slab):
                    rows = pl.ds(k * BF16_ROWS, BF16_ROWS)
                    contrib = jnp.where(r2_ref[h, tc, rows, :] < nb, e2_ref[h, tc, rows, :] * cb, 0)
                    w[k] = contrib if w[k] is None else w[k] + contrib
            for k in range(n_slab):
                rows = pl.ds(r * PEER_NKEYS + k * BF16_ROWS, BF16_ROWS)
                at_ref[tc, rows, :] = _twice_gelu_bf16(hid_ref[tc, rows, :], c0, c1) * w[k]


def _peer_kernel(h1_ref, gf_ref, wq1t_ref, wq2t_ref, kperm_ref, keys2_ref, u_ref, vt_ref, gfin_ref, y_ref,
                 xnt_ref, s1t_ref, c1t_ref, n1t_ref, s2_ref, r2_ref, e2_ref, a_ref, b_ref,
                 hid_ref, at_ref, acc_ref, n1s_ref, c1s_ref, *, tb, eb):
    s = pl.program_id(1)
    rows_per_cell = eb // PEER_NKEYS
    n_tc = tb // LANES

    @pl.when(s == 0)
    def _():
        xn = _rms(h1_ref[...], gf_ref[...])
        xnt_ref[...] = xn.T.astype(BF16)
        nq = PEER_HEADS * PEER_HALF
        rc = 256
        assert eb >= 2 * nq
        for r0 in range(0, nq, rc):
            for w_ref, base in ((wq1t_ref, 0), (wq2t_ref, nq)):
                q = jnp.dot(w_ref[r0:r0 + rc, :], xnt_ref[...], preferred_element_type=F32).astype(BF16)
                for tc in range(n_tc):
                    hid_ref[tc, base + r0:base + r0 + rc, :] = q[:, tc * LANES:(tc + 1) * LANES]
        q1t = jnp.concatenate([hid_ref[tc, 0:nq, :] for tc in range(n_tc)], axis=1)
        for r0 in range(0, nq, rc):
            s1c = jnp.dot(kperm_ref[r0:r0 + rc, :], q1t, preferred_element_type=F32)
            s1t_ref[r0 // PEER_HEADS:(r0 + rc) // PEER_HEADS] = s1c.reshape(rc // PEER_HEADS, PEER_HEADS, tb)
        for h in range(PEER_HEADS):
            rows = slice(nq + h * PEER_HALF, nq + (h + 1) * PEER_HALF)
            q2h = jnp.concatenate([hid_ref[tc, rows, :] for tc in range(n_tc)], axis=1)
            s2_ref[h] = jnp.dot(keys2_ref[h], q2h, preferred_element_type=F32)
        def route(tc, carry):
            _peer_route_chunk(tc, s1t_ref, s2_ref, c1t_ref, n1t_ref, r2_ref, e2_ref, a_ref, b_ref)
            return carry

        lax.fori_loop(0, tb // LANES, route, 0)
        acc_ref[...] = jnp.zeros_like(acc_ref)

    first = pl.multiple_of(s * rows_per_cell, rows_per_cell)
    n1s_ref[...] = n1t_ref[pl.ds(first, rows_per_cell)]
    c1s_ref[...] = c1t_ref[pl.ds(first, rows_per_cell)]
    for m0 in range(0, eb, PEER_MM1_ROWS):
        hid = jnp.dot(u_ref[m0:m0 + PEER_MM1_ROWS, :], xnt_ref[...], preferred_element_type=F32).astype(BF16)
        for tc in range(n_tc):
            hid_ref[tc, m0:m0 + PEER_MM1_ROWS, :] = hid[:, tc * LANES:(tc + 1) * LANES]
    _peer_elementwise(hid_ref, at_ref, n1s_ref, c1s_ref, r2_ref, e2_ref, tb=tb, eb=eb)
    at_full = jnp.concatenate([at_ref[tc] for tc in range(n_tc)], axis=1)
    acc_ref[...] += jnp.dot(vt_ref[...], at_full, preferred_element_type=F32)

    @pl.when(s == pl.num_programs(1) - 1)
    def _():
        h2 = h1_ref[...] + acc_ref[...].T
        y_ref[...] = _rms(h2, gfin_ref[...])


def _peer(h1, g_ffn, wq1t_bf, wq2t_bf, kperm_bf, keys2_bf, u_bf, vt_bf, g_final):
    t, d = h1.shape
    n_exp = u_bf.shape[0]
    tb, eb = 512, 2048
    full = lambda a: pl.BlockSpec(a.shape, lambda i, s: (0,) * a.ndim)
    route_t = pltpu.VMEM((PEER_NKEYS, PEER_HEADS, tb), F32)
    ranks = pltpu.VMEM((PEER_TOPK, PEER_HEADS, tb), F32)

    def chunked(*lead):
        return pltpu.VMEM(lead[:-1] + (tb // LANES, lead[-1], LANES), BF16)

    return pl.pallas_call(
        functools.partial(_peer_kernel, tb=tb, eb=eb),
        grid=(t // tb, n_exp // eb),
        in_specs=[pl.BlockSpec((tb, d), lambda i, s: (i, 0), pipeline_mode=pl.Buffered(1)),
                  full(g_ffn), full(wq1t_bf), full(wq2t_bf), full(kperm_bf), full(keys2_bf),
                  pl.BlockSpec((eb, d), lambda i, s: (s, 0)),
                  pl.BlockSpec((d, eb), lambda i, s: (0, s)),
                  full(g_final)],
        out_specs=pl.BlockSpec((tb, d), lambda i, s: (i, 0), pipeline_mode=pl.Buffered(1)),
        out_shape=jax.ShapeDtypeStruct((t, d), F32),
        scratch_shapes=[pltpu.VMEM((d, tb), BF16),
                        route_t,
                        route_t,
                        route_t,
                        pltpu.VMEM((PEER_HEADS, PEER_NKEYS, tb), F32),
                        chunked(PEER_HEADS, PEER_NKEYS),
                        chunked(PEER_HEADS, PEER_NKEYS),
                        ranks, ranks,
                        chunked(eb),
                        chunked(eb),
                        pltpu.VMEM((d, tb), F32),
                        pltpu.VMEM((eb // PEER_NKEYS, PEER_HEADS, tb), F32),
                        pltpu.VMEM((eb // PEER_NKEYS, PEER_HEADS, tb), F32)],
        compiler_params=pltpu.CompilerParams(dimension_semantics=("arbitrary", "arbitrary"),
                                             vmem_limit_bytes=PEER_VMEM_LIMIT),
    )(h1, g_ffn, wq1t_bf, wq2t_bf, kperm_bf, keys2_bf, u_bf, vt_bf, g_final)


def _block_diag(blocks):
    g, r, c = blocks.shape
    eye = jnp.eye(g, dtype=blocks.dtype)
    return jnp.einsum("grc,gk->grkc", blocks, eye).reshape(g * r, g * c)


def _peer_params(w_q, keys):
    d = w_q.shape[0]
    wq = w_q.reshape(d, PEER_HEADS, 2, PEER_HALF)
    wq1t = wq[:, :, 0, :].reshape(d, PEER_HEADS * PEER_HALF).T
    wq2t = wq[:, :, 1, :].reshape(d, PEER_HEADS * PEER_HALF).T
    eye = jnp.eye(PEER_HEADS, dtype=keys.dtype)
    kperm = jnp.einsum("hid,hk->ihkd", keys[:, 0], eye).reshape(PEER_NKEYS * PEER_HEADS, PEER_HEADS * PEER_HALF)
    return wq1t.astype(BF16), wq2t.astype(BF16), kperm.astype(BF16), keys[:, 1].astype(BF16)


def kernel(x, mem, positions, norm_mix, w_in, norm_mem, w_mem_kv, s5_a_re, s5_a_im, s5_log_dt, s5_b_re, s5_b_im, s5_c_re, s5_c_im, s5_d, s5_w_glu, w_branch_ret, w_branch_s5, w_branch_mem, w_out, norm_ffn, peer_w_q, peer_keys, peer_u, peer_v, norm_final):
    bsz, seq, d = x.shape
    depth = w_in.shape[0]
    assert depth == 1, "the final RMSNorm is fused into the PEER call of a single layer"
    assert bsz % SUBLANES == 0 and seq % 512 == 0 and d == 1024
    half = RET_DK // 2
    inv_freq = (ROPE_BASE ** (-jnp.arange(half, dtype=F32) / half)).reshape(1, half)
    posf = positions.astype(F32).reshape(bsz, seq, 1)
    row = lambda v: v.reshape(1, -1).astype(F32)

    h = x
    for l in range(depth):
        lam_re, lam_im, bb_re, bb_im = _s5_discretize(s5_a_re[l], s5_a_im[l], s5_log_dt[l], s5_b_re[l], s5_b_im[l])
        bmat = jnp.concatenate([_block_diag(bb_re), _block_diag(bb_im)], axis=1).astype(BF16)
        lam = jnp.stack([lam_re.reshape(-1), lam_im.reshape(-1)])
        c_re_t = jnp.transpose(s5_c_re[l], (0, 2, 1))
        c_im_t = jnp.transpose(s5_c_im[l], (0, 2, 1))
        cmat = jnp.concatenate([_block_diag(c_re_t), -_block_diag(c_im_t)], axis=0).astype(BF16)

        z = _inproj(h.reshape(bsz * seq, d), row(norm_mix[l]), w_in[l].astype(BF16))
        z3 = z.reshape(bsz, seq, IN_COLS)
        ret = _retention(z3, posf, inv_freq)
        u_sb = jnp.transpose(z3[:, :, COL_U:COL_U + S5_WIDTH], (1, 0, 2)).astype(F32)
        ssm_sb = _s5(u_sb, bmat, lam, cmat, row(s5_d[l]), s5_w_glu[l].astype(BF16))
        ssm = jnp.transpose(ssm_sb, (1, 0, 2))
        kv = _memkv(mem, row(norm_mem[l]), w_mem_kv[l].astype(BF16))
        h1 = _merge(h, ret, ssm, z3, kv, w_branch_ret[l].astype(BF16), w_branch_s5[l].astype(BF16),
                    w_branch_mem[l].astype(BF16), w_out[l].astype(BF16))
        wq1t, wq2t, kperm, keys2 = _peer_params(peer_w_q[l], peer_keys[l])
        y = _peer(h1.reshape(bsz * seq, d), row(norm_ffn[l]), wq1t, wq2t, kperm, keys2,
                  peer_u[l].astype(BF16), peer_v[l].T.astype(BF16), row(norm_final))
        h = y.reshape(bsz, seq, d)
    return h
```
